```python
import jax, jax.numpy as jnp
from jax import lax
import numpy as np

D_MODEL = 1024
BATCH = 8
SEQ = 4096
DEPTH = 1

HEAD_DIM = 128
FOX_HEADS = 4
DSA_HEADS = 4
IDX_HEADS = 8
IDX_DIM = 64
ROT_FRAC_DEN = 4
ROPE_THETA = 500000.0
TOPK_MAX = 256
Q_BLOCK = 128
D_FF = -(-8 * D_MODEL // (3 * 256)) * 256
RMS_EPS = 1e-6
N_BRANCHES = 2

FOX_W = FOX_HEADS * HEAD_DIM
DSA_W = DSA_HEADS * HEAD_DIM
IN_WIDTHS = (FOX_W, FOX_W, FOX_W, FOX_HEADS, DSA_W, HEAD_DIM, HEAD_DIM,
             IDX_HEADS * IDX_DIM, IDX_DIM, IDX_HEADS, D_MODEL, D_MODEL)
D_IN = sum(IN_WIDTHS)
IN_SPLITS = tuple(int(v) for v in np.cumsum(IN_WIDTHS)[:-1])

kernel_name = "fox_dsa_gated_hybrid_block"


def rmsnorm(x, g):
    xf = x.astype(jnp.float32)
    y = xf * lax.rsqrt(jnp.mean(xf * xf, axis=-1, keepdims=True) + RMS_EPS)
    return (y * g.astype(jnp.float32)).astype(x.dtype)


def rope_partial(x, pos, rot):
    half = rot // 2
    inv_freq = jnp.float32(ROPE_THETA) ** (-jnp.arange(half, dtype=jnp.float32) * 2.0 / rot)
    ang = pos.astype(jnp.float32)[:, None] * inv_freq[None, :]
    if x.ndim == 4:
        ang = ang[:, None, :]
    cos = jnp.cos(ang).astype(x.dtype)
    sin = jnp.sin(ang).astype(x.dtype)
    x1, x2, rest = x[..., :half], x[..., half:rot], x[..., rot:]
    return jnp.concatenate([x1 * cos - x2 * sin, x2 * cos + x1 * sin, rest], axis=-1)


def fox_attention(q, k, v, log_f):
    B, S, H, dh = q.shape
    scale = dh ** -0.5
    c = jnp.cumsum(log_f, axis=1).transpose(0, 2, 1)
    kpos = jnp.arange(S)

    def block(i):
        start = i * Q_BLOCK
        q_b = lax.dynamic_slice_in_dim(q, start, Q_BLOCK, axis=1)
        c_b = lax.dynamic_slice_in_dim(c, start, Q_BLOCK, axis=2)
        s = jnp.einsum('bqhd,bkhd->bhqk', q_b, k).astype(jnp.float32) * scale
        bias = c_b[..., :, None] - c[..., None, :]
        qpos = start + jnp.arange(Q_BLOCK)
        causal = kpos[None, :] <= qpos[:, None]
        s = jnp.where(causal, s + bias, -jnp.inf)
        p = jax.nn.softmax(s, axis=-1)
        return jnp.einsum('bhqk,bkhd->bqhd', p.astype(v.dtype), v)

    out = lax.map(block, jnp.arange(S // Q_BLOCK))
    return out.transpose(1, 0, 2, 3, 4).reshape(B, S, H, dh)


def dsa_attention(q, k, v, q_idx, k_idx, w_idx, topk):
    B, S, H, dh = q.shape
    scale = dh ** -0.5
    idx_scale = (IDX_DIM ** -0.5) * (IDX_HEADS ** -0.5)
    kpos = jnp.arange(S)
    gather = jax.vmap(lambda arr, ii: arr[ii])

    def block(i):
        start = i * Q_BLOCK
        qpos = start + jnp.arange(Q_BLOCK)
        q_b = lax.dynamic_slice_in_dim(q, start, Q_BLOCK, axis=1)
        qi_b = lax.dynamic_slice_in_dim(q_idx, start, Q_BLOCK, axis=1)
        wi_b = lax.dynamic_slice_in_dim(w_idx, start, Q_BLOCK, axis=1)
        dots = jnp.einsum('bqhd,bkd->bqhk', qi_b, k_idx).astype(jnp.float32)
        score = jnp.einsum('bqhk,bqh->bqk', jax.nn.relu(dots), wi_b.astype(jnp.float32)) * idx_scale
        causal = kpos[None, :] <= qpos[:, None]
        score = jnp.where(causal[None], score, -jnp.inf)
        _, sel = lax.top_k(score, topk)
        valid = sel <= qpos[None, :, None]
        k_g = gather(k, sel)
        v_g = gather(v, sel)
        s = jnp.einsum('bqhd,bqkd->bhqk', q_b, k_g).astype(jnp.float32) * scale
        s = jnp.where(valid[:, None], s, -jnp.inf)
        p = jax.nn.softmax(s, axis=-1)
        return jnp.einsum('bhqk,bqkd->bqhd', p.astype(v.dtype), v_g)

    out = lax.map(block, jnp.arange(S // Q_BLOCK))
    return out.transpose(1, 0, 2, 3, 4).reshape(B, S, H, dh)


def setup_inputs(seed: int = 0) -> dict:
    key = jax.random.key(seed)
    ks = jax.random.split(key, 16)
    f32 = jnp.float32
    nrm = lambda k, shape, fan: jax.random.normal(k, shape, f32) * (fan ** -0.5)
    gain = lambda k: 1.0 + 0.1 * jax.random.normal(k, (DEPTH, D_MODEL), f32)
    return {
        "x": jax.random.normal(ks[0], (BATCH, SEQ, D_MODEL), f32),
        "norm_mix_pre": gain(ks[1]),
        "w_in": nrm(ks[2], (DEPTH, D_MODEL, D_IN), D_MODEL),
        "b_forget": 2.0 + 0.1 * jax.random.normal(ks[3], (DEPTH, FOX_HEADS), f32),
        "b_gate": 0.1 * jax.random.normal(ks[4], (DEPTH, N_BRANCHES, D_MODEL), f32),
        "w_branch_fox": nrm(ks[5], (DEPTH, FOX_W, D_MODEL), FOX_W),
        "w_branch_dsa": nrm(ks[6], (DEPTH, DSA_W, D_MODEL), DSA_W),
        "w_out": nrm(ks[7], (DEPTH, D_MODEL, D_MODEL), D_MODEL),
        "norm_mix_post": gain(ks[8]),
        "norm_ffn_pre": gain(ks[9]),
        "w_ffn_gate": nrm(ks[10], (DEPTH, D_MODEL, D_FF), D_MODEL),
        "w_ffn_up": nrm(ks[11], (DEPTH, D_MODEL, D_FF), D_MODEL),
        "w_ffn_down": nrm(ks[12], (DEPTH, D_FF, D_MODEL), D_FF),
        "norm_ffn_post": gain(ks[13]),
    }


def reference(x, norm_mix_pre, w_in, b_forget, b_gate, w_branch_fox, w_branch_dsa, w_out,
              norm_mix_post, norm_ffn_pre, w_ffn_gate, w_ffn_up, w_ffn_down, norm_ffn_post):
    B, S, _ = x.shape
    pos = jnp.arange(S)
    topk = min(TOPK_MAX, S // 4)
    rot_h = HEAD_DIM // ROT_FRAC_DEN
    rot_i = IDX_DIM // ROT_FRAC_DEN
    h = x
    for l in range(DEPTH):
        u = rmsnorm(h, norm_mix_pre[l])
        proj = u @ w_in[l]
        (q_f, k_f, v_f, f_logit, q_d, k_d, v_d,
         q_i, k_i, w_i, g_f, g_d) = jnp.split(proj, IN_SPLITS, axis=-1)

        log_f = jax.nn.log_sigmoid(f_logit.astype(jnp.float32) + b_forget[l].astype(jnp.float32))
        o_f = fox_attention(q_f.reshape(B, S, FOX_HEADS, HEAD_DIM),
                            k_f.reshape(B, S, FOX_HEADS, HEAD_DIM),
                            v_f.reshape(B, S, FOX_HEADS, HEAD_DIM), log_f).reshape(B, S, FOX_W)

        q_d = rope_partial(q_d.reshape(B, S, DSA_HEADS, HEAD_DIM), pos, rot_h)
        k_d = rope_partial(k_d, pos, rot_h)
        q_i = rope_partial(q_i.reshape(B, S, IDX_HEADS, IDX_DIM), pos, rot_i)
        k_i = rope_partial(k_i, pos, rot_i)
        o_d = dsa_attention(q_d, k_d, v_d, q_i, k_i, w_i, topk).reshape(B, S, DSA_W)

        mixed = (jax.nn.sigmoid(g_f + b_gate[l, 0]) * (o_f @ w_branch_fox[l])
                 + jax.nn.sigmoid(g_d + b_gate[l, 1]) * (o_d @ w_branch_dsa[l]))
        h = h + rmsnorm(mixed @ w_out[l], norm_mix_post[l])

        v_in = rmsnorm(h, norm_ffn_pre[l])
        ff = (jax.nn.silu(v_in @ w_ffn_gate[l]) * (v_in @ w_ffn_up[l])) @ w_ffn_down[l]
        h = h + rmsnorm(ff, norm_ffn_post[l])
    return h
```

```python
import functools

import jax
import jax.numpy as jnp
from jax import lax
from jax.experimental import pallas as pl
from jax.experimental.pallas import tpu as pltpu

D_MODEL = 1024
HEAD_DIM = 128
FOX_HEADS = 4
DSA_HEADS = 4
IDX_HEADS = 8
IDX_DIM = 64
ROT_FRAC_DEN = 4
ROPE_THETA = 500000.0
TOPK_MAX = 256
D_FF = 2816
RMS_EPS = 1e-6
FOX_W = FOX_HEADS * HEAD_DIM
DSA_W = DSA_HEADS * HEAD_DIM
IDX_W = IDX_HEADS * IDX_DIM
IN_WIDTHS = (FOX_W, FOX_W, FOX_W, FOX_HEADS, DSA_W, HEAD_DIM, HEAD_DIM,
             IDX_W, IDX_DIM, IDX_HEADS, D_MODEL, D_MODEL)

LANES = 128
VMEM_LIMIT = 56 * 1024 * 1024
MM_DTYPE = jnp.bfloat16
NEG_BIG = -1e30

F32 = jnp.float32


def _nt_dot(a, b):
    return lax.dot_general(a, b, (((1,), (1,)), ((), ())), preferred_element_type=F32)


def _dot(a, b):
    return jnp.dot(a, b, preferred_element_type=F32)


def _rms(x, g):
    return x * lax.rsqrt(jnp.mean(x * x, axis=-1, keepdims=True) + RMS_EPS) * g


def _sigmoid(z):
    return 1.0 / (1.0 + jnp.exp(-z))


MISC_F0 = IDX_HEADS


def _rope(seg, cos, sin_a, sin_b, half):
    return (seg * cos
            + pltpu.roll(seg, LANES - half, axis=1) * sin_a
            + pltpu.roll(seg, half, axis=1) * sin_b)


def _input_kernel(x_ref, g_ref, wf_ref, wd_ref, wi_ref, wm_ref, wg_ref, bm_ref, bg_ref,
                  cd_ref, sad_ref, sbd_ref, ci_ref, sai_ref, sbi_ref,
                  f_ref, d_ref, i_ref, m_ref, gate_ref, carry_ref, *, idx_scale):
    tm = x_ref.shape[0]
    u = _rms(x_ref[...], g_ref[...]).astype(MM_DTYPE)

    f_ref[...] = _dot(u, wf_ref[...]).astype(f_ref.dtype)

    yd = _dot(u, wd_ref[...])
    cd, sad, sbd = cd_ref[...], sad_ref[...], sbd_ref[...]
    n_rot_d = (DSA_W + HEAD_DIM) // LANES
    for j in range(yd.shape[1] // LANES):
        seg = yd[:, j * LANES:(j + 1) * LANES]
        if j < n_rot_d:
            seg = _rope(seg, cd, sad, sbd, HEAD_DIM // ROT_FRAC_DEN // 2)
        d_ref[:, j * LANES:(j + 1) * LANES] = seg.astype(d_ref.dtype)

    yi = _dot(u, wi_ref[...])
    ci, sai, sbi = ci_ref[...], sai_ref[...], sbi_ref[...]
    for j in range(yi.shape[1] // LANES):
        seg = _rope(yi[:, j * LANES:(j + 1) * LANES], ci, sai, sbi, IDX_DIM // ROT_FRAC_DEN // 2)
        i_ref[:, j * LANES:(j + 1) * LANES] = seg.astype(i_ref.dtype)

    gate_ref[...] = _sigmoid(_dot(u, wg_ref[...]) + bg_ref[...]).astype(gate_ref.dtype)

    ym = _dot(u, wm_ref[...])
    col = lax.broadcasted_iota(jnp.int32, ym.shape, 1)
    row = lax.broadcasted_iota(jnp.int32, ym.shape, 0)
    is_f = (col >= MISC_F0) & (col < MISC_F0 + FOX_HEADS)
    z = ym + bm_ref[...]
    log_f = jnp.where(is_f, jnp.minimum(z, 0.0) - jnp.log1p(jnp.exp(-jnp.abs(z))), 0.0)

    @pl.when(pl.program_id(1) == 0)
    def _():
        carry_ref[...] = jnp.zeros_like(carry_ref)

    c = log_f
    shift = 1
    while shift < tm:
        c = c + jnp.where(row >= shift, pltpu.roll(c, shift, axis=0), 0.0)
        shift *= 2
    c = c + carry_ref[0:1, :]
    carry_ref[0:1, :] = c[tm - 1:tm, :]
    m_ref[...] = jnp.where(is_f, c, ym * idx_scale)


def _input_stage(x, gain, wf, wd, wi, wm, wg, bm, bg, tabs, *, tm):
    B, S, D = x.shape
    idx_scale = (IDX_DIM ** -0.5) * (IDX_HEADS ** -0.5)
    full = lambda a: pl.BlockSpec(a.shape, lambda b, t: (0,) * a.ndim)
    tab = pl.BlockSpec((tm, LANES), lambda b, t: (t, 0))
    out_w = (wf.shape[1], wd.shape[1], wi.shape[1], wm.shape[1], wg.shape[1])
    out_dt = (MM_DTYPE, MM_DTYPE, MM_DTYPE, F32, MM_DTYPE)
    return pl.pallas_call(
        functools.partial(_input_kernel, idx_scale=idx_scale),
        grid=(B, S // tm),
        in_specs=[pl.BlockSpec((None, tm, D), lambda b, t: (b, t, 0)),
                  full(gain), full(wf), full(wd), full(wi), full(wm), full(wg), full(bm), full(bg)]
                 + [tab] * 6,
        out_specs=[pl.BlockSpec((None, tm, w), lambda b, t: (b, t, 0)) for w in out_w],
        out_shape=[jax.ShapeDtypeStruct((B, S, w), dt) for w, dt in zip(out_w, out_dt)],
        scratch_shapes=[pltpu.VMEM((8, LANES), F32)],
        compiler_params=pltpu.CompilerParams(
            dimension_semantics=("arbitrary", "arbitrary"), vmem_limit_bytes=VMEM_LIMIT),
        name="input_stage",
    )(x, gain, wf, wd, wi, wm, wg, bm, bg, *tabs)


def _fox_kernel(q_ref, k_ref, v_ref, ct_ref, o_ref, *, scale):
    h = pl.program_id(1)
    i = pl.program_id(2)
    tq = q_ref.shape[0]
    q = q_ref[...]

    def step(j, carry, masked):
        m, l, acc = carry
        start = pl.multiple_of(j * tq, tq)
        s = _nt_dot(q, k_ref[pl.ds(start, tq), :]) * scale - ct_ref[pl.ds(h, 1), pl.ds(start, tq)]
        if masked:
            row = lax.broadcasted_iota(jnp.int32, s.shape, 0)
            col = lax.broadcasted_iota(jnp.int32, s.shape, 1)
            s = jnp.where(col <= row, s, NEG_BIG)
        m_new = jnp.maximum(m, jnp.max(s, axis=-1, keepdims=True))
        alpha = jnp.exp(m - m_new)
        p = jnp.exp(s - m_new)
        l = alpha * l + jnp.sum(p, axis=-1, keepdims=True)
        acc = alpha * acc + _dot(p.astype(MM_DTYPE), v_ref[pl.ds(start, tq), :])
        return m_new, l, acc

    init = (jnp.full((tq, 1), NEG_BIG, F32), jnp.zeros((tq, 1), F32),
            jnp.zeros((tq, HEAD_DIM), F32))
    carry = lax.fori_loop(0, i, lambda j, c: step(j, c, False), init)
    _, l, acc = step(i, carry, True)
    o_ref[...] = (acc / l).astype(o_ref.dtype)


def _fox_attention(qkv, ct, *, tq):
    B, S, _ = qkv.shape
    H = FOX_HEADS
    return pl.pallas_call(
        functools.partial(_fox_kernel, scale=HEAD_DIM ** -0.5),
        grid=(B, H, S // tq),
        in_specs=[pl.BlockSpec((None, tq, HEAD_DIM), lambda b, h, i: (b, i, h)),
                  pl.BlockSpec((None, S, HEAD_DIM), lambda b, h, i: (b, 0, H + h)),
                  pl.BlockSpec((None, S, HEAD_DIM), lambda b, h, i: (b, 0, 2 * H + h)),
                  pl.BlockSpec((None, H, S), lambda b, h, i: (b, 0, 0))],
        out_specs=pl.BlockSpec((None, tq, HEAD_DIM), lambda b, h, i: (b, i, h)),
        out_shape=jax.ShapeDtypeStruct((B, S, FOX_W), MM_DTYPE),
        compiler_params=pltpu.CompilerParams(
            dimension_semantics=("arbitrary", "arbitrary", "arbitrary"),
            vmem_limit_bytes=VMEM_LIMIT),
        name="fox_attention",
    )(qkv, qkv, qkv, ct)


DSA_CHUNK = 512
F32_INF_BITS = 0x7F800000
IDX_BITS = 13


def _lane_fold(x):
    acc = x[:, 0:LANES]
    for j in range(1, x.shape[1] // LANES):
        acc = acc + x[:, j * LANES:(j + 1) * LANES]
    return acc


def _dsa_kernel(qi_ref, ki_ref, m_ref, qd_ref, kd_ref, vd_ref, o_ref, sc_ref, jt_ref,
                *, scale, topk):
    i = pl.program_id(1)
    tq = qi_ref.shape[0]
    ch = DSA_CHUNK
    n_ch = (i * tq + tq + ch - 1) // ch
    q_pos = i * tq + lax.broadcasted_iota(jnp.int32, (tq, ch), 0)
    lane = lax.broadcasted_iota(jnp.int32, (tq, ch), 1)

    half_lane = lax.broadcasted_iota(jnp.int32, (tq, LANES), 1) // IDX_DIM
    q_heads = []
    for h in range(IDX_HEADS):
        pair = qi_ref[:, (h // 2) * LANES:(h // 2 + 1) * LANES]
        q_heads.append(jnp.where(half_lane == h % 2, pair, jnp.zeros_like(pair)))
    w_all = m_ref[...]

    def score_chunk(c, _):
        start = pl.multiple_of(c * ch, ch)
        kc = ki_ref[pl.ds(start, ch), :]
        acc = jnp.zeros((tq, ch), F32)
        for h in range(IDX_HEADS):
            acc = acc + jnp.maximum(_nt_dot(q_heads[h], kc), 0.0) * w_all[:, h:h + 1]
        sc_ref[:, pl.ds(start, ch)] = jnp.where(start + lane <= q_pos, acc, -jnp.inf)
        return 0

    lax.fori_loop(0, n_ch, score_chunk, 0)

    def count(pred):
        def body(c, cnt):
            start = pl.multiple_of(c * ch, ch)
            hit = pred(sc_ref[:, pl.ds(start, ch)], start)
            return cnt + _lane_fold(jnp.where(hit, 1.0, 0.0))
        part = lax.fori_loop(0, n_ch, body, jnp.zeros((tq, LANES), F32))
        return jnp.sum(part, axis=-1, keepdims=True)

    def count_ge(t):
        tb = jnp.broadcast_to(t, (tq, ch))
        return count(lambda s, _: s >= tb)

    kf = jnp.float32(topk)

    neg = count_ge(jnp.zeros((tq, 1), F32)) < kf
    pos = jnp.logical_not(neg)

    def as_thresh(bits):
        mag = pltpu.bitcast(bits, F32)
        return jnp.where(neg, -mag, mag)

    def bit_step(n, bits):
        cand = bits | (jnp.int32(1) << (30 - n))
        cnt = count_ge(as_thresh(cand))
        take = (neg & (cnt < kf) & (cand <= F32_INF_BITS)) | (pos & (cnt >= kf))
        return jnp.where(take, cand, bits)

    bits = lax.fori_loop(0, 31, bit_step, jnp.zeros((tq, 1), jnp.int32))
    t = as_thresh(jnp.where(neg, bits + 1, bits))
    tb = jnp.broadcast_to(t, (tq, ch))

    n_gt = count(lambda s, _: s > tb)
    n_ge = count_ge(t)
    need = kf - n_gt
    jt_ref[...] = jnp.full(jt_ref.shape, 2 ** IDX_BITS - 1, jnp.int32)

    @pl.when(jnp.max(n_ge) > kf)
    def _():
        def idx_step(n, jt):
            cand = jt | (jnp.int32(1) << (IDX_BITS - 1 - n))
            cb = jnp.broadcast_to(cand, (tq, ch))
            below = count(lambda s, start: (s == tb) & (start + lane < cb))
            return jnp.where(below <= need, cand, jt)
        jt = lax.fori_loop(0, IDX_BITS, idx_step, jnp.zeros((tq, 1), jnp.int32))
        jt_ref[...] = jnp.broadcast_to(jt, jt_ref.shape)

    jb = jnp.broadcast_to(jt_ref[:, 0:1], (tq, ch))

    q_d = [qd_ref[:, h * HEAD_DIM:(h + 1) * HEAD_DIM] for h in range(DSA_HEADS)]

    def attn_chunk(c, carry):
        start = pl.multiple_of(c * ch, ch)
        s_idx = sc_ref[:, pl.ds(start, ch)]
        pos = start + lane
        sel = ((s_idx > tb) | ((s_idx == tb) & (pos < jb))) & (pos <= q_pos)
        bias = jnp.where(sel, 0.0, NEG_BIG)
        kc = kd_ref[pl.ds(start, ch), :]
        vc = vd_ref[pl.ds(start, ch), :]
        out = []
        for h in range(DSA_HEADS):
            m, l, acc = carry[h]
            s = _nt_dot(q_d[h], kc) * scale + bias
            m_new = jnp.maximum(m, jnp.max(s, axis=-1, keepdims=True))
            alpha = jnp.exp(m - m_new)
            p = jnp.exp(s - m_new)
            l = alpha * l + jnp.sum(p, axis=-1, keepdims=True)
            acc = alpha * acc + _dot(p.astype(MM_DTYPE), vc)
            out.append((m_new, l, acc))
        return tuple(out)

    init = tuple((jnp.full((tq, 1), NEG_BIG, F32), jnp.zeros((tq, 1), F32),
                  jnp.zeros((tq, HEAD_DIM), F32)) for _ in range(DSA_HEADS))
    res = lax.fori_loop(0, n_ch, attn_chunk, init)
    for h in range(DSA_HEADS):
        _, l, acc = res[h]
        o_ref[:, h * HEAD_DIM:(h + 1) * HEAD_DIM] = (acc / l).astype(o_ref.dtype)


def _dsa_attention(qk_i, misc, qkv_d, *, tq, topk):
    B, S, _ = qkv_d.shape
    s_pad = -(-S // DSA_CHUNK) * DSA_CHUNK
    blk = lambda w, col: pl.BlockSpec((None, tq, w), lambda b, i: (b, i, col))
    res = lambda col: pl.BlockSpec((None, S, LANES), lambda b, i: (b, 0, col))
    return pl.pallas_call(
        functools.partial(_dsa_kernel, scale=HEAD_DIM ** -0.5, topk=topk),
        grid=(B, S // tq),
        in_specs=[blk(IDX_W, 0), res(IDX_W // LANES), blk(LANES, 0),
                  blk(DSA_W, 0), res(DSA_W // LANES), res(DSA_W // LANES + 1)],
        out_specs=blk(DSA_W, 0),
        out_shape=jax.ShapeDtypeStruct((B, S, DSA_W), MM_DTYPE),
        scratch_shapes=[pltpu.VMEM((tq, s_pad), F32), pltpu.VMEM((tq, LANES), jnp.int32)],
        compiler_params=pltpu.CompilerParams(
            dimension_semantics=("arbitrary", "arbitrary"), vmem_limit_bytes=VMEM_LIMIT),
        name="dsa_attention",
    )(qk_i, qk_i, misc, qkv_d, qkv_d, qkv_d)


def _merge_kernel(of_ref, od_ref, gate_ref, x_ref, wbf_ref, wbd_ref, wo_ref, g_ref, h_ref):
    a = _dot(of_ref[...], wbf_ref[...])
    b = _dot(od_ref[...], wbd_ref[...])
    mixed = (gate_ref[:, 0:D_MODEL].astype(F32) * a
             + gate_ref[:, D_MODEL:2 * D_MODEL].astype(F32) * b)
    y = _dot(mixed.astype(MM_DTYPE), wo_ref[...])
    h_ref[...] = x_ref[...] + _rms(y, g_ref[...])


def _merge_stage(o_f, o_d, gates, x, wbf, wbd, wo, g_post, *, tm):
    N, D = x.shape
    row = lambda w: pl.BlockSpec((tm, w), lambda t: (t, 0))
    full = lambda a: pl.BlockSpec(a.shape, lambda t: (0,) * a.ndim)
    return pl.pallas_call(
        _merge_kernel,
        grid=(N // tm,),
        in_specs=[row(FOX_W), row(DSA_W), row(2 * D), row(D),
                  full(wbf), full(wbd), full(wo), full(g_post)],
        out_specs=row(D),
        out_shape=jax.ShapeDtypeStruct((N, D), F32),
        compiler_params=pltpu.CompilerParams(
            dimension_semantics=("arbitrary",), vmem_limit_bytes=VMEM_LIMIT),
        name="merge_stage",
    )(o_f, o_d, gates, x, wbf, wbd, wo, g_post)


def _ffn_kernel(h_ref, gpre_ref, wg_ref, wu_ref, wd_ref, gpost_ref, o_ref, v_ref, acc_ref):
    j = pl.program_id(1)

    @pl.when(j == 0)
    def _():
        v_ref[...] = _rms(h_ref[...], gpre_ref[...]).astype(v_ref.dtype)
        acc_ref[...] = jnp.zeros_like(acc_ref)

    v = v_ref[...]
    g = _dot(v, wg_ref[...])
    a = (g * _sigmoid(g) * _dot(v, wu_ref[...])).astype(MM_DTYPE)
    acc_ref[...] += _dot(a, wd_ref[...])

    @pl.when(j == pl.num_programs(1) - 1)
    def _():
        o_ref[...] = h_ref[...] + _rms(acc_ref[...], gpost_ref[...])


def _ffn_stage(h, g_pre, wg, wu, wd, g_post, *, tm, tf):
    N, D = h.shape
    F = wg.shape[1]
    full = lambda a: pl.BlockSpec(a.shape, lambda t, j: (0,) * a.ndim)
    return pl.pallas_call(
        _ffn_kernel,
        grid=(N // tm, F // tf),
        in_specs=[pl.BlockSpec((tm, D), lambda t, j: (t, 0)), full(g_pre),
                  pl.BlockSpec((D, tf), lambda t, j: (0, j)),
                  pl.BlockSpec((D, tf), lambda t, j: (0, j)),
                  pl.BlockSpec((tf, D), lambda t, j: (j, 0)), full(g_post)],
        out_specs=pl.BlockSpec((tm, D), lambda t, j: (t, 0)),
        out_shape=jax.ShapeDtypeStruct((N, D), F32),
        scratch_shapes=[pltpu.VMEM((tm, D), MM_DTYPE), pltpu.VMEM((tm, D), F32)],
        compiler_params=pltpu.CompilerParams(
            dimension_semantics=("arbitrary", "arbitrary"), vmem_limit_bytes=VMEM_LIMIT),
        name="ffn_stage",
    )(h, g_pre, wg, wu, wd, g_post)


def _rope_tables(S, rot, period):
    half = rot // 2
    inv_freq = jnp.float32(ROPE_THETA) ** (-jnp.arange(half, dtype=F32) * 2.0 / rot)
    ang = jnp.arange(S).astype(F32)[:, None] * inv_freq[None, :]
    cos, sin = jnp.cos(ang), jnp.sin(ang)
    one = jnp.ones((S, period - rot), F32)
    zero = jnp.zeros((S, period - rot), F32)
    zh = jnp.zeros((S, half), F32)
    reps = LANES // period
    cos_t = jnp.tile(jnp.concatenate([cos, cos, one], axis=1), (1, reps))
    sin_a = jnp.tile(jnp.concatenate([-sin, zh, zero], axis=1), (1, reps))
    sin_b = jnp.tile(jnp.concatenate([zh, sin, zero], axis=1), (1, reps))
    return cos_t, sin_a, sin_b


def _split_w_in(w):
    parts, off = [], 0
    for width in IN_WIDTHS:
        parts.append(w[:, off:off + width])
        off += width
    return parts


def _layer(h, p, tabs, *, tm_in, tq_fox, tq_dsa, tm_merge, tm_ffn, tf):
    B, S, D = h.shape
    (w_qf, w_kf, w_vf, w_fl, w_qd, w_kd, w_vd, w_qi, w_ki, w_wi, w_gf, w_gd) = _split_w_in(p["w_in"])
    cast = lambda a: a.astype(MM_DTYPE)
    pad_m = jnp.zeros((D, LANES - IDX_HEADS - FOX_HEADS), F32)
    wf = cast(jnp.concatenate([w_qf, w_kf, w_vf], axis=1))
    wd = cast(jnp.concatenate([w_qd, w_kd, w_vd], axis=1))
    wi = cast(jnp.concatenate([w_qi, w_ki, w_ki], axis=1))
    wm = cast(jnp.concatenate([w_wi, w_fl, pad_m], axis=1))
    wg = cast(jnp.concatenate([w_gf, w_gd], axis=1))
    bm = jnp.zeros((1, LANES), F32).at[0, MISC_F0:MISC_F0 + FOX_HEADS].set(p["b_forget"].astype(F32))
    bg = p["b_gate"].astype(F32).reshape(1, 2 * D)
    row = lambda a: a.astype(F32).reshape(1, D)

    qkv_f, qkv_d, qk_i, misc, gates = _input_stage(
        h, row(p["norm_mix_pre"]), wf, wd, wi, wm, wg, bm, bg, tabs, tm=tm_in)

    ct = jnp.transpose(misc[:, :, MISC_F0:MISC_F0 + FOX_HEADS], (0, 2, 1))
    o_f = _fox_attention(qkv_f, ct, tq=tq_fox)
    o_d = _dsa_attention(qk_i, misc, qkv_d, tq=tq_dsa, topk=min(TOPK_MAX, S // 4))

    N = B * S
    h1 = _merge_stage(o_f.reshape(N, FOX_W), o_d.reshape(N, DSA_W), gates.reshape(N, 2 * D),
                      h.reshape(N, D), cast(p["w_branch_fox"]), cast(p["w_branch_dsa"]),
                      cast(p["w_out"]), row(p["norm_mix_post"]), tm=tm_merge)
    h2 = _ffn_stage(h1, row(p["norm_ffn_pre"]), cast(p["w_ffn_gate"]), cast(p["w_ffn_up"]),
                    cast(p["w_ffn_down"]), row(p["norm_ffn_post"]), tm=tm_ffn, tf=tf)
    return h2.reshape(B, S, D)


def kernel(x, norm_mix_pre, w_in, b_forget, b_gate, w_branch_fox, w_branch_dsa, w_out,
           norm_mix_post, norm_ffn_pre, w_ffn_gate, w_ffn_up, w_ffn_down, norm_ffn_post):
    B, S, D = x.shape
    params = dict(norm_mix_pre=norm_mix_pre, w_in=w_in, b_forget=b_forget, b_gate=b_gate,
                  w_branch_fox=w_branch_fox, w_branch_dsa=w_branch_dsa, w_out=w_out,
                  norm_mix_post=norm_mix_post, norm_ffn_pre=norm_ffn_pre, w_ffn_gate=w_ffn_gate,
                  w_ffn_up=w_ffn_up, w_ffn_down=w_ffn_down, norm_ffn_post=norm_ffn_post)
    tabs = (_rope_tables(S, HEAD_DIM // ROT_FRAC_DEN, HEAD_DIM)
            + _rope_tables(S, IDX_DIM // ROT_FRAC_DEN, IDX_DIM))
    tiles = dict(tm_in=min(512, S), tq_fox=min(512, S), tq_dsa=128,
                 tm_merge=min(512, B * S), tm_ffn=min(1024, B * S), tf=D_FF // 2)
    h = x
    for l in range(w_in.shape[0]):
        h = _layer(h, {k: v[l] for k, v in params.items()}, tabs, **tiles)
    return h
```

```python
import functools

import jax
import jax.numpy as jnp
from jax import lax
from jax.experimental import pallas as pl
from jax.experimental.pallas import tpu as pltpu

D_MODEL = 1024
HEAD_DIM = 128
FOX_HEADS = 4
DSA_HEADS = 4
IDX_HEADS = 8
IDX_DIM = 64
ROT_FRAC_DEN = 4
ROPE_THETA = 500000.0
TOPK_MAX = 256
D_FF = 2816
RMS_EPS = 1e-6
FOX_W = FOX_HEADS * HEAD_DIM
DSA_W = DSA_HEADS * HEAD_DIM
IDX_W = IDX_HEADS * IDX_DIM
IN_WIDTHS = (FOX_W, FOX_W, FOX_W, FOX_HEADS, DSA_W, HEAD_DIM, HEAD_DIM,
             IDX_W, IDX_DIM, IDX_HEADS, D_MODEL, D_MODEL)

LANES = 128
SUBLANES = 8
VMEM_LIMIT = 56 * 1024 * 1024
MM_DTYPE = jnp.bfloat16
NEG_BIG = -1e30
LOG2E = 1.4426950408889634

F32 = jnp.float32


def _nt_dot(a, b):
    return lax.dot_general(a, b, (((1,), (1,)), ((), ())), preferred_element_type=F32)


def _dot(a, b):
    return jnp.dot(a, b, preferred_element_type=F32)


def _rms(x, g):
    return x * lax.rsqrt(jnp.mean(x * x, axis=-1, keepdims=True) + RMS_EPS) * g


def _sigmoid(z):
    return 1.0 / (1.0 + jnp.exp(-z))


MISC_F0 = IDX_HEADS


def _rope(seg, cos, sin_a, sin_b, half):
    return (seg * cos
            + pltpu.roll(seg, LANES - half, axis=1) * sin_a
            + pltpu.roll(seg, half, axis=1) * sin_b)


def _input_kernel(x_ref, g_ref, wf_ref, wd_ref, wi_ref, wm_ref, wg_ref, bm_ref, bg_ref,
                  cd_ref, sad_ref, sbd_ref, ci_ref, sai_ref, sbi_ref,
                  f_ref, d_ref, vt_ref, i_ref, m_ref, gate_ref, carry_ref, *, idx_scale, q_scale):
    tm = x_ref.shape[0]
    u = _rms(x_ref[...], g_ref[...]).astype(MM_DTYPE)

    f_ref[...] = _dot(u, wf_ref[...]).astype(f_ref.dtype)

    yd = _dot(u, wd_ref[...])
    cd, sad, sbd = cd_ref[...], sad_ref[...], sbd_ref[...]
    for j in range(DSA_HEADS + 1):
        seg = _rope(yd[:, j * LANES:(j + 1) * LANES], cd, sad, sbd, HEAD_DIM // ROT_FRAC_DEN // 2)
        if j < DSA_HEADS:
            seg = seg * q_scale
        d_ref[:, j * LANES:(j + 1) * LANES] = seg.astype(d_ref.dtype)
    vt_ref[...] = yd[:, (DSA_HEADS + 1) * LANES:].T.astype(vt_ref.dtype)

    yi = _dot(u, wi_ref[...])
    ci, sai, sbi = ci_ref[...], sai_ref[...], sbi_ref[...]
    for j in range(yi.shape[1] // LANES):
        seg = _rope(yi[:, j * LANES:(j + 1) * LANES], ci, sai, sbi, IDX_DIM // ROT_FRAC_DEN // 2)
        i_ref[:, j * LANES:(j + 1) * LANES] = seg.astype(i_ref.dtype)

    gate_ref[...] = _sigmoid(_dot(u, wg_ref[...]) + bg_ref[...]).astype(gate_ref.dtype)

    ym = _dot(u, wm_ref[...])
    col = lax.broadcasted_iota(jnp.int32, ym.shape, 1)
    row = lax.broadcasted_iota(jnp.int32, ym.shape, 0)
    is_f = (col >= MISC_F0) & (col < MISC_F0 + FOX_HEADS)
    z = ym + bm_ref[...]
    log_f = jnp.where(is_f, jnp.minimum(z, 0.0) - jnp.log1p(jnp.exp(-jnp.abs(z))), 0.0)

    @pl.when(pl.program_id(1) == 0)
    def _():
        carry_ref[...] = jnp.zeros_like(carry_ref)

    c = log_f
    shift = 1
    while shift < tm:
        c = c + jnp.where(row >= shift, pltpu.roll(c, shift, axis=0), 0.0)
        shift *= 2
    c = c + carry_ref[0:1, :]
    carry_ref[0:1, :] = c[tm - 1:tm, :]
    m_ref[...] = jnp.where(is_f, c, ym * idx_scale)


def _input_stage(x, gain, wf, wd, wi, wm, wg, bm, bg, tabs, *, tm):
    B, S, D = x.shape
    idx_scale = (IDX_DIM ** -0.5) * (IDX_HEADS ** -0.5)
    full = lambda a: pl.BlockSpec(a.shape, lambda b, t: (0,) * a.ndim)
    tab = pl.BlockSpec((tm, LANES), lambda b, t: (t, 0))
    rows = lambda w: pl.BlockSpec((None, tm, w), lambda b, t: (b, t, 0))
    qk_d_w = DSA_W + HEAD_DIM
    return pl.pallas_call(
        functools.partial(_input_kernel, idx_scale=idx_scale, q_scale=HEAD_DIM ** -0.5 * LOG2E),
        grid=(B, S // tm),
        in_specs=[rows(D), full(gain), full(wf), full(wd), full(wi), full(wm), full(wg),
                  full(bm), full(bg)] + [tab] * 6,
        out_specs=[rows(wf.shape[1]), rows(qk_d_w),
                   pl.BlockSpec((None, HEAD_DIM, tm), lambda b, t: (b, 0, t)),
                   rows(wi.shape[1]), rows(wm.shape[1]), rows(wg.shape[1])],
        out_shape=[jax.ShapeDtypeStruct((B, S, wf.shape[1]), MM_DTYPE),
                   jax.ShapeDtypeStruct((B, S, qk_d_w), MM_DTYPE),
                   jax.ShapeDtypeStruct((B, HEAD_DIM, S), MM_DTYPE),
                   jax.ShapeDtypeStruct((B, S, wi.shape[1]), MM_DTYPE),
                   jax.ShapeDtypeStruct((B, S, wm.shape[1]), F32),
                   jax.ShapeDtypeStruct((B, S, wg.shape[1]), MM_DTYPE)],
        scratch_shapes=[pltpu.VMEM((8, LANES), F32)],
        compiler_params=pltpu.CompilerParams(
            dimension_semantics=("arbitrary", "arbitrary"), vmem_limit_bytes=VMEM_LIMIT),
        name="input_stage",
    )(x, gain, wf, wd, wi, wm, wg, bm, bg, *tabs)


def _fox_kernel(q_ref, k_ref, v_ref, ct_ref, o_ref, *, scale):
    h = pl.program_id(1)
    i = pl.program_id(2)
    tq = q_ref.shape[0]
    q = q_ref[...]

    def step(j, carry, masked):
        m, l, acc = carry
        start = pl.multiple_of(j * tq, tq)
        s = _nt_dot(q, k_ref[pl.ds(start, tq), :]) * scale - ct_ref[pl.ds(h, 1), pl.ds(start, tq)]
        if masked:
            row = lax.broadcasted_iota(jnp.int32, s.shape, 0)
            col = lax.broadcasted_iota(jnp.int32, s.shape, 1)
            s = jnp.where(col <= row, s, NEG_BIG)
        m_new = jnp.maximum(m, jnp.max(s, axis=-1, keepdims=True))
        alpha = jnp.exp(m - m_new)
        p = jnp.exp(s - m_new)
        l = alpha * l + jnp.sum(p, axis=-1, keepdims=True)
        acc = alpha * acc + _dot(p.astype(MM_DTYPE), v_ref[pl.ds(start, tq), :])
        return m_new, l, acc

    init = (jnp.full((tq, 1), NEG_BIG, F32), jnp.zeros((tq, 1), F32),
            jnp.zeros((tq, HEAD_DIM), F32))
    carry = lax.fori_loop(0, i, lambda j, c: step(j, c, False), init)
    _, l, acc = step(i, carry, True)
    o_ref[...] = (acc / l).astype(o_ref.dtype)


def _fox_attention(qkv, ct, *, tq):
    B, S, _ = qkv.shape
    H = FOX_HEADS
    return pl.pallas_call(
        functools.partial(_fox_kernel, scale=HEAD_DIM ** -0.5),
        grid=(B, H, S // tq),
        in_specs=[pl.BlockSpec((None, tq, HEAD_DIM), lambda b, h, i: (b, i, h)),
                  pl.BlockSpec((None, S, HEAD_DIM), lambda b, h, i: (b, 0, H + h)),
                  pl.BlockSpec((None, S, HEAD_DIM), lambda b, h, i: (b, 0, 2 * H + h)),
                  pl.BlockSpec((None, H, S), lambda b, h, i: (b, 0, 0))],
        out_specs=pl.BlockSpec((None, tq, HEAD_DIM), lambda b, h, i: (b, i, h)),
        out_shape=jax.ShapeDtypeStruct((B, S, FOX_W), MM_DTYPE),
        compiler_params=pltpu.CompilerParams(
            dimension_semantics=("arbitrary", "arbitrary", "arbitrary"),
            vmem_limit_bytes=VMEM_LIMIT),
        name="fox_attention",
    )(qkv, qkv, qkv, ct)


F32_INF_BITS = 0x7F800000
IDX_BITS = 13


def _sublane_fold(x):
    acc = x[0:SUBLANES, :]
    for j in range(1, x.shape[0] // SUBLANES):
        acc = acc + x[j * SUBLANES:(j + 1) * SUBLANES, :]
    return acc


def _dsa_kernel(qi_ref, ki_ref, m_ref, qd_ref, kd_ref, vt_ref, o_ref,
                sc_ref, jt_ref, m_s, l_s, acc_s, *, topk):
    i = pl.program_id(1)
    tq = qi_ref.shape[0]
    ch = tq
    n_ch = i + 1
    key_off = lax.broadcasted_iota(jnp.int32, (ch, tq), 0)
    q_off = lax.broadcasted_iota(jnp.int32, (ch, tq), 1)
    causal_diag = key_off <= q_off

    half_lane = lax.broadcasted_iota(jnp.int32, (tq, LANES), 1) // IDX_DIM
    q_heads = []
    for h in range(IDX_HEADS):
        pair = qi_ref[:, (h // 2) * LANES:(h // 2 + 1) * LANES]
        q_heads.append(jnp.where(half_lane == h % 2, pair, jnp.zeros_like(pair)))
    w_t = m_ref[...].T

    def score_chunk(c, diag):
        start = pl.multiple_of(c * ch, ch)
        kc = ki_ref[pl.ds(start, ch), :]
        acc = jnp.zeros((ch, tq), F32)
        for h in range(IDX_HEADS):
            acc = acc + jnp.maximum(_nt_dot(kc, q_heads[h]), 0.0) * w_t[h:h + 1, :]
        if diag:
            acc = jnp.where(causal_diag, acc, -jnp.inf)
        sc_ref[pl.ds(start, ch), :] = acc

    def score_body(c, _):
        score_chunk(c, False)
        return 0

    lax.fori_loop(0, i, score_body, 0)
    score_chunk(i, True)

    def count(pred):
        def body(c, cnt):
            start = pl.multiple_of(c * ch, ch)
            hit = pred(sc_ref[pl.ds(start, ch), :], start)
            return cnt + _sublane_fold(jnp.where(hit, 1.0, 0.0))
        part = lax.fori_loop(0, n_ch, body, jnp.zeros((SUBLANES, tq), F32))
        return jnp.sum(part, axis=0, keepdims=True)

    def count_ge(t):
        tb = jnp.broadcast_to(t, (ch, tq))
        return count(lambda s, _: s >= tb)

    kf = jnp.float32(topk)

    neg = count_ge(jnp.zeros((1, tq), F32)) < kf
    pos = jnp.logical_not(neg)

    def as_thresh(bits):
        mag = pltpu.bitcast(bits, F32)
        return jnp.where(neg, -mag, mag)

    def bit_step(n, bits):
        cand = bits | (jnp.int32(1) << (30 - n))
        cnt = count_ge(as_thresh(cand))
        take = (neg & (cnt < kf) & (cand <= F32_INF_BITS)) | (pos & (cnt >= kf))
        return jnp.where(take, cand, bits)

    bits = lax.fori_loop(0, 31, bit_step, jnp.zeros((1, tq), jnp.int32))
    t = as_thresh(jnp.where(neg, bits + 1, bits))
    tb = jnp.broadcast_to(t, (ch, tq))

    n_gt = count(lambda s, _: s > tb)
    n_ge = count_ge(t)
    need = kf - n_gt
    jt_ref[...] = jnp.full(jt_ref.shape, 2 ** IDX_BITS - 1, jnp.int32)

    @pl.when(jnp.max(n_ge) > kf)
    def _():
        def idx_step(n, jt):
            cand = jt | (jnp.int32(1) << (IDX_BITS - 1 - n))
            cb = jnp.broadcast_to(cand, (ch, tq))
            below = count(lambda s, start: (s == tb) & (start + key_off < cb))
            return jnp.where(below <= need, cand, jt)
        jt = lax.fori_loop(0, IDX_BITS, idx_step, jnp.zeros((1, tq), jnp.int32))
        jt_ref[...] = jnp.broadcast_to(jt, jt_ref.shape)

    jb = jnp.broadcast_to(jt_ref[0:1, :], (ch, tq))

    m_s[...] = jnp.full(m_s.shape, NEG_BIG, F32)
    l_s[...] = jnp.zeros(l_s.shape, F32)
    acc_s[...] = jnp.zeros(acc_s.shape, F32)
    q_all = jnp.concatenate([qd_ref[:, h * HEAD_DIM:(h + 1) * HEAD_DIM]
                             for h in range(DSA_HEADS)], axis=0)

    def attn_chunk(c, diag):
        start = pl.multiple_of(c * ch, ch)
        s_idx = sc_ref[pl.ds(start, ch), :]
        sel = (s_idx > tb) | ((s_idx == tb) & (start + key_off < jb))
        if diag:
            sel = sel & causal_diag
        s = _nt_dot(kd_ref[pl.ds(start, ch), :], q_all)
        s = jnp.concatenate([jnp.where(sel, s[:, h * tq:(h + 1) * tq], NEG_BIG)
                             for h in range(DSA_HEADS)], axis=1)
        m_old = m_s[0:1, :]
        m_new = jnp.maximum(m_old, jnp.max(s, axis=0, keepdims=True))
        alpha = jnp.exp2(m_old - m_new)
        p = jnp.exp2(s - m_new)
        l_s[0:1, :] = alpha * l_s[0:1, :] + jnp.sum(p, axis=0, keepdims=True)
        acc_s[...] = alpha * acc_s[...] + _dot(vt_ref[:, pl.ds(start, ch)], p.astype(MM_DTYPE))
        m_s[0:1, :] = m_new

    def attn_body(c, _):
        attn_chunk(c, False)
        return 0

    lax.fori_loop(0, i, attn_body, 0)
    attn_chunk(i, True)

    for h in range(DSA_HEADS):
        cols = slice(h * tq, (h + 1) * tq)
        o_t = acc_s[:, cols] / l_s[0:1, cols]
        o_ref[:, h * HEAD_DIM:(h + 1) * HEAD_DIM] = o_t.T.astype(o_ref.dtype)


def _dsa_attention(qk_i, misc, qk_d, v_t, *, tq, topk):
    B, S, _ = qk_d.shape
    blk = lambda w, col: pl.BlockSpec((None, tq, w), lambda b, i: (b, i, col))
    res = lambda col: pl.BlockSpec((None, S, LANES), lambda b, i: (b, 0, col))
    return pl.pallas_call(
        functools.partial(_dsa_kernel, topk=topk),
        grid=(B, S // tq),
        in_specs=[blk(IDX_W, 0), res(IDX_W // LANES), blk(LANES, 0),
                  blk(DSA_W, 0), res(DSA_W // LANES),
                  pl.BlockSpec((None, HEAD_DIM, S), lambda b, i: (b, 0, 0))],
        out_specs=blk(DSA_W, 0),
        out_shape=jax.ShapeDtypeStruct((B, S, DSA_W), MM_DTYPE),
        scratch_shapes=[pltpu.VMEM((S, tq), F32),
                        pltpu.VMEM((SUBLANES, tq), jnp.int32),
                        pltpu.VMEM((SUBLANES, DSA_HEADS * tq), F32),
                        pltpu.VMEM((SUBLANES, DSA_HEADS * tq), F32),
                        pltpu.VMEM((HEAD_DIM, DSA_HEADS * tq), F32)],
        compiler_params=pltpu.CompilerParams(
            dimension_semantics=("arbitrary", "arbitrary"), vmem_limit_bytes=VMEM_LIMIT),
        name="dsa_attention",
    )(qk_i, qk_i, misc, qk_d, qk_d, v_t)


def _merge_kernel(of_ref, od_ref, gate_ref, x_ref, wbf_ref, wbd_ref, wo_ref, g_ref, h_ref):
    a = _dot(of_ref[...], wbf_ref[...])
    b = _dot(od_ref[...], wbd_ref[...])
    mixed = (gate_ref[:, 0:D_MODEL].astype(F32) * a
             + gate_ref[:, D_MODEL:2 * D_MODEL].astype(F32) * b)
    y = _dot(mixed.astype(MM_DTYPE), wo_ref[...])
    h_ref[...] = x_ref[...] + _rms(y, g_ref[...])


def _merge_stage(o_f, o_d, gates, x, wbf, wbd, wo, g_post, *, tm):
    N, D = x.shape
    row = lambda w: pl.BlockSpec((tm, w), lambda t: (t, 0))
    full = lambda a: pl.BlockSpec(a.shape, lambda t: (0,) * a.ndim)
    return pl.pallas_call(
        _merge_kernel,
        grid=(N // tm,),
        in_specs=[row(FOX_W), row(DSA_W), row(2 * D), row(D),
                  full(wbf), full(wbd), full(wo), full(g_post)],
        out_specs=row(D),
        out_shape=jax.ShapeDtypeStruct((N, D), F32),
        compiler_params=pltpu.CompilerParams(
            dimension_semantics=("arbitrary",), vmem_limit_bytes=VMEM_LIMIT),
        name="merge_stage",
    )(o_f, o_d, gates, x, wbf, wbd, wo, g_post)


def _ffn_kernel(h_ref, gpre_ref, wg_ref, wu_ref, wd_ref, gpost_ref, o_ref, v_ref, acc_ref):
    j = pl.program_id(1)

    @pl.when(j == 0)
    def _():
        v_ref[...] = _rms(h_ref[...], gpre_ref[...]).astype(v_ref.dtype)
        acc_ref[...] = jnp.zeros_like(acc_ref)

    v = v_ref[...]
    g = _dot(v, wg_ref[...])
    a = (g * _sigmoid(g) * _dot(v, wu_ref[...])).astype(MM_DTYPE)
    acc_ref[...] += _dot(a, wd_ref[...])

    @pl.when(j == pl.num_programs(1) - 1)
    def _():
        o_ref[...] = h_ref[...] + _rms(acc_ref[...], gpost_ref[...])


def _ffn_stage(h, g_pre, wg, wu, wd, g_post, *, tm, tf):
    N, D = h.shape
    F = wg.shape[1]
    full = lambda a: pl.BlockSpec(a.shape, lambda t, j: (0,) * a.ndim)
    return pl.pallas_call(
        _ffn_kernel,
        grid=(N // tm, F // tf),
        in_specs=[pl.BlockSpec((tm, D), lambda t, j: (t, 0)), full(g_pre),
                  pl.BlockSpec((D, tf), lambda t, j: (0, j)),
                  pl.BlockSpec((D, tf), lambda t, j: (0, j)),
                  pl.BlockSpec((tf, D), lambda t, j: (j, 0)), full(g_post)],
        out_specs=pl.BlockSpec((tm, D), lambda t, j: (t, 0)),
        out_shape=jax.ShapeDtypeStruct((N, D), F32),
        scratch_shapes=[pltpu.VMEM((tm, D), MM_DTYPE), pltpu.VMEM((tm, D), F32)],
        compiler_params=pltpu.CompilerParams(
            dimension_semantics=("arbitrary", "arbitrary"), vmem_limit_bytes=VMEM_LIMIT),
        name="ffn_stage",
    )(h, g_pre, wg, wu, wd, g_post)


def _rope_tables(S, rot, period):
    half = rot // 2
    inv_freq = jnp.float32(ROPE_THETA) ** (-jnp.arange(half, dtype=F32) * 2.0 / rot)
    ang = jnp.arange(S).astype(F32)[:, None] * inv_freq[None, :]
    cos, sin = jnp.cos(ang), jnp.sin(ang)
    one = jnp.ones((S, period - rot), F32)
    zero = jnp.zeros((S, period - rot), F32)
    zh = jnp.zeros((S, half), F32)
    reps = LANES // period
    cos_t = jnp.tile(jnp.concatenate([cos, cos, one], axis=1), (1, reps))
    sin_a = jnp.tile(jnp.concatenate([-sin, zh, zero], axis=1), (1, reps))
    sin_b = jnp.tile(jnp.concatenate([zh, sin, zero], axis=1), (1, reps))
    return cos_t, sin_a, sin_b


def _split_w_in(w):
    parts, off = [], 0
    for width in IN_WIDTHS:
        parts.append(w[:, off:off + width])
        off += width
    return parts


def _layer(h, p, tabs, *, tm_in, tq_fox, tq_dsa, tm_merge, tm_ffn, tf):
    B, S, D = h.shape
    (w_qf, w_kf, w_vf, w_fl, w_qd, w_kd, w_vd, w_qi, w_ki, w_wi, w_gf, w_gd) = _split_w_in(p["w_in"])
    cast = lambda a: a.astype(MM_DTYPE)
    pad_m = jnp.zeros((D, LANES - IDX_HEADS - FOX_HEADS), F32)
    wf = cast(jnp.concatenate([w_qf, w_kf, w_vf], axis=1))
    wd = cast(jnp.concatenate([w_qd, w_kd, w_vd], axis=1))
    wi = cast(jnp.concatenate([w_qi, w_ki, w_ki], axis=1))
    wm = cast(jnp.concatenate([w_wi, w_fl, pad_m], axis=1))
    wg = cast(jnp.concatenate([w_gf, w_gd], axis=1))
    bm = jnp.zeros((1, LANES), F32).at[0, MISC_F0:MISC_F0 + FOX_HEADS].set(p["b_forget"].astype(F32))
    bg = p["b_gate"].astype(F32).reshape(1, 2 * D)
    row = lambda a: a.astype(F32).reshape(1, D)

    qkv_f, qk_d, v_t, qk_i, misc, gates = _input_stage(
        h, row(p["norm_mix_pre"]), wf, wd, wi, wm, wg, bm, bg, tabs, tm=tm_in)

    ct = jnp.transpose(misc[:, :, MISC_F0:MISC_F0 + FOX_HEADS], (0, 2, 1))
    o_f = _fox_attention(qkv_f, ct, tq=tq_fox)
    o_d = _dsa_attention(qk_i, misc, qk_d, v_t, tq=tq_dsa, topk=min(TOPK_MAX, S // 4))

    N = B * S
    h1 = _merge_stage(o_f.reshape(N, FOX_W), o_d.reshape(N, DSA_W), gates.reshape(N, 2 * D),
                      h.reshape(N, D), cast(p["w_branch_fox"]), cast(p["w_branch_dsa"]),
                      cast(p["w_out"]), row(p["norm_mix_post"]), tm=tm_merge)
    h2 = _ffn_stage(h1, row(p["norm_ffn_pre"]), cast(p["w_ffn_gate"]), cast(p["w_ffn_up"]),
                    cast(p["w_ffn_down"]), row(p["norm_ffn_post"]), tm=tm_ffn, tf=tf)
    return h2.reshape(B, S, D)


def kernel(x, norm_mix_pre, w_in, b_forget, b_gate, w_branch_fox, w_branch_dsa, w_out,
           norm_mix_post, norm_ffn_pre, w_ffn_gate, w_ffn_up, w_ffn_down, norm_ffn_post):
    B, S, D = x.shape
    params = dict(norm_mix_pre=norm_mix_pre, w_in=w_in, b_forget=b_forget, b_gate=b_gate,
                  w_branch_fox=w_branch_fox, w_branch_dsa=w_branch_dsa, w_out=w_out,
                  norm_mix_post=norm_mix_post, norm_ffn_pre=norm_ffn_pre, w_ffn_gate=w_ffn_gate,
                  w_ffn_up=w_ffn_up, w_ffn_down=w_ffn_down, norm_ffn_post=norm_ffn_post)
    tabs = (_rope_tables(S, HEAD_DIM // ROT_FRAC_DEN, HEAD_DIM)
            + _rope_tables(S, IDX_DIM // ROT_FRAC_DEN, IDX_DIM))
    tiles = dict(tm_in=min(512, S), tq_fox=min(512, S), tq_dsa=256,
                 tm_merge=min(512, B * S), tm_ffn=min(1024, B * S), tf=D_FF // 2)
    h = x
    for l in range(w_in.shape[0]):
        h = _layer(h, {k: v[l] for k, v in params.items()}, tabs, **tiles)
    return h
```

```python
import functools

import jax
import jax.numpy as jnp
from jax import lax
from jax.experimental import pallas as pl
from jax.experimental.pallas import tpu as pltpu

D_MODEL = 1024
HEAD_DIM = 128
FOX_HEADS = 4
DSA_HEADS = 4
IDX_HEADS = 8
IDX_DIM = 64
ROT_FRAC_DEN = 4
ROPE_THETA = 500000.0
TOPK_MAX = 256
D_FF = 2816
RMS_EPS = 1e-6
FOX_W = FOX_HEADS * HEAD_DIM
DSA_W = DSA_HEADS * HEAD_DIM
IDX_W = IDX_HEADS * IDX_DIM
IN_WIDTHS = (FOX_W, FOX_W, FOX_W, FOX_HEADS, DSA_W, HEAD_DIM, HEAD_DIM,
             IDX_W, IDX_DIM, IDX_HEADS, D_MODEL, D_MODEL)

LANES = 128
SUBLANES = 8
VMEM_LIMIT = 56 * 1024 * 1024
MM_DTYPE = jnp.bfloat16
NEG_BIG = -1e30
LOG2E = 1.4426950408889634
Q_SCALE = HEAD_DIM ** -0.5 * LOG2E

F32 = jnp.float32


def _nt_dot(a, b):
    return lax.dot_general(a, b, (((1,), (1,)), ((), ())), preferred_element_type=F32)


def _dot(a, b):
    return jnp.dot(a, b, preferred_element_type=F32)


def _rms(x, g):
    return x * lax.rsqrt(jnp.mean(x * x, axis=-1, keepdims=True) + RMS_EPS) * g


def _sigmoid(z):
    return 1.0 / (1.0 + jnp.exp(-z))


def _sublane_fold(x, op):
    acc = x[0:SUBLANES, :]
    for j in range(1, x.shape[0] // SUBLANES):
        acc = op(acc, x[j * SUBLANES:(j + 1) * SUBLANES, :])
    return acc


MISC_F0 = IDX_HEADS


def _rope(seg, cos, sin_a, sin_b, half):
    return (seg * cos
            + pltpu.roll(seg, LANES - half, axis=1) * sin_a
            + pltpu.roll(seg, half, axis=1) * sin_b)


def _input_kernel(x_ref, g_ref, wf_ref, wd_ref, wi_ref, wm_ref, wg_ref, bm_ref, bg_ref,
                  cd_ref, sad_ref, sbd_ref, ci_ref, sai_ref, sbi_ref,
                  f_ref, vtf_ref, d_ref, vtd_ref, i_ref, m_ref, gate_ref, carry_ref, *, idx_scale):
    tm = x_ref.shape[0]
    u = _rms(x_ref[...], g_ref[...]).astype(MM_DTYPE)

    yf = _dot(u, wf_ref[...])
    f_ref[:, 0:FOX_W] = (yf[:, 0:FOX_W] * Q_SCALE).astype(f_ref.dtype)
    f_ref[:, FOX_W:2 * FOX_W] = yf[:, FOX_W:2 * FOX_W].astype(f_ref.dtype)
    for h in range(FOX_HEADS):
        v = yf[:, 2 * FOX_W + h * HEAD_DIM:2 * FOX_W + (h + 1) * HEAD_DIM]
        vtf_ref[h * HEAD_DIM:(h + 1) * HEAD_DIM, :] = v.T.astype(vtf_ref.dtype)

    yd = _dot(u, wd_ref[...])
    cd, sad, sbd = cd_ref[...], sad_ref[...], sbd_ref[...]
    for j in range(DSA_HEADS + 1):
        seg = _rope(yd[:, j * LANES:(j + 1) * LANES], cd, sad, sbd, HEAD_DIM // ROT_FRAC_DEN // 2)
        if j < DSA_HEADS:
            seg = seg * Q_SCALE
        d_ref[:, j * LANES:(j + 1) * LANES] = seg.astype(d_ref.dtype)
    vtd_ref[...] = yd[:, (DSA_HEADS + 1) * LANES:].T.astype(vtd_ref.dtype)

    yi = _dot(u, wi_ref[...])
    ci, sai, sbi = ci_ref[...], sai_ref[...], sbi_ref[...]
    for j in range(yi.shape[1] // LANES):
        seg = _rope(yi[:, j * LANES:(j + 1) * LANES], ci, sai, sbi, IDX_DIM // ROT_FRAC_DEN // 2)
        i_ref[:, j * LANES:(j + 1) * LANES] = seg.astype(i_ref.dtype)

    gate_ref[...] = _sigmoid(_dot(u, wg_ref[...]) + bg_ref[...]).astype(gate_ref.dtype)

    ym = _dot(u, wm_ref[...])
    col = lax.broadcasted_iota(jnp.int32, ym.shape, 1)
    row = lax.broadcasted_iota(jnp.int32, ym.shape, 0)
    is_f = (col >= MISC_F0) & (col < MISC_F0 + FOX_HEADS)
    z = ym + bm_ref[...]
    log_f = jnp.where(is_f, jnp.minimum(z, 0.0) - jnp.log1p(jnp.exp(-jnp.abs(z))), 0.0)

    @pl.when(pl.program_id(1) == 0)
    def _():
        carry_ref[...] = jnp.zeros_like(carry_ref)

    c = log_f
    shift = 1
    while shift < tm:
        c = c + jnp.where(row >= shift, pltpu.roll(c, shift, axis=0), 0.0)
        shift *= 2
    c = c + carry_ref[0:1, :]
    carry_ref[0:1, :] = c[tm - 1:tm, :]
    m_ref[...] = jnp.where(is_f, c * LOG2E, ym * idx_scale)


def _input_stage(x, gain, wf, wd, wi, wm, wg, bm, bg, tabs, *, tm):
    B, S, D = x.shape
    idx_scale = (IDX_DIM ** -0.5) * (IDX_HEADS ** -0.5)
    full = lambda a: pl.BlockSpec(a.shape, lambda b, t: (0,) * a.ndim)
    tab = pl.BlockSpec((tm, LANES), lambda b, t: (t, 0))
    rows = lambda w: pl.BlockSpec((None, tm, w), lambda b, t: (b, t, 0))
    cols = lambda h: pl.BlockSpec((None, h, tm), lambda b, t: (b, 0, t))
    qk_d_w = DSA_W + HEAD_DIM
    return pl.pallas_call(
        functools.partial(_input_kernel, idx_scale=idx_scale),
        grid=(B, S // tm),
        in_specs=[rows(D), full(gain), full(wf), full(wd), full(wi), full(wm), full(wg),
                  full(bm), full(bg)] + [tab] * 6,
        out_specs=[rows(2 * FOX_W), cols(FOX_W), rows(qk_d_w), cols(HEAD_DIM),
                   rows(wi.shape[1]), rows(wm.shape[1]), rows(wg.shape[1])],
        out_shape=[jax.ShapeDtypeStruct((B, S, 2 * FOX_W), MM_DTYPE),
                   jax.ShapeDtypeStruct((B, FOX_W, S), MM_DTYPE),
                   jax.ShapeDtypeStruct((B, S, qk_d_w), MM_DTYPE),
                   jax.ShapeDtypeStruct((B, HEAD_DIM, S), MM_DTYPE),
                   jax.ShapeDtypeStruct((B, S, wi.shape[1]), MM_DTYPE),
                   jax.ShapeDtypeStruct((B, S, wm.shape[1]), F32),
                   jax.ShapeDtypeStruct((B, S, wg.shape[1]), MM_DTYPE)],
        scratch_shapes=[pltpu.VMEM((8, LANES), F32)],
        compiler_params=pltpu.CompilerParams(
            dimension_semantics=("arbitrary", "arbitrary"), vmem_limit_bytes=VMEM_LIMIT),
        name="input_stage",
    )(x, gain, wf, wd, wi, wm, wg, bm, bg, *tabs)


def _softmax_init(m_s, l_s, acc_s):
    m_s[...] = jnp.full(m_s.shape, NEG_BIG, F32)
    l_s[...] = jnp.zeros(l_s.shape, F32)
    acc_s[...] = jnp.zeros(acc_s.shape, F32)


def _softmax_step(s, pv_fn, m_s, l_s, acc_s):
    m_old = m_s[0:1, :]
    m_new = jnp.maximum(m_old, jnp.max(s, axis=0, keepdims=True))
    alpha = jnp.exp2(m_old - m_new)
    p = jnp.exp2(s - m_new)
    l_s[0:1, :] = alpha * l_s[0:1, :] + jnp.sum(p, axis=0, keepdims=True)
    acc_s[...] = alpha * acc_s[...] + pv_fn(p.astype(MM_DTYPE))
    m_s[0:1, :] = m_new


def _softmax_finish(o_ref, l_s, acc_s, heads, tq):
    for h in range(heads):
        cols = slice(h * tq, (h + 1) * tq)
        o_t = acc_s[:, cols] / l_s[0:1, cols]
        o_ref[:, h * HEAD_DIM:(h + 1) * HEAD_DIM] = o_t.T.astype(o_ref.dtype)


def _softmax_scratch(heads, tq):
    return [pltpu.VMEM((SUBLANES, heads * tq), F32),
            pltpu.VMEM((SUBLANES, heads * tq), F32),
            pltpu.VMEM((HEAD_DIM, heads * tq), F32)]


def _fox_kernel(q_ref, k_ref, vt_ref, m_ref, o_ref, m_s, l_s, acc_s):
    i = pl.program_id(1)
    tq = q_ref.shape[0]
    ch = tq
    H = FOX_HEADS
    causal_diag = (lax.broadcasted_iota(jnp.int32, (ch, tq), 0)
                   <= lax.broadcasted_iota(jnp.int32, (ch, tq), 1))
    _softmax_init(m_s, l_s, acc_s)

    def chunk(c, diag):
        start = pl.multiple_of(c * ch, ch)
        parts = []
        for h in range(H):
            hd = slice(h * HEAD_DIM, (h + 1) * HEAD_DIM)
            s = _nt_dot(k_ref[pl.ds(start, ch), hd], q_ref[:, hd])
            s = s - m_ref[pl.ds(start, ch), MISC_F0 + h:MISC_F0 + h + 1]
            if diag:
                s = jnp.where(causal_diag, s, NEG_BIG)
            parts.append(s)

        def pv(p):
            return jnp.concatenate(
                [_dot(vt_ref[h * HEAD_DIM:(h + 1) * HEAD_DIM, pl.ds(start, ch)],
                      p[:, h * tq:(h + 1) * tq]) for h in range(H)], axis=1)

        _softmax_step(jnp.concatenate(parts, axis=1), pv, m_s, l_s, acc_s)

    def body(c, _):
        chunk(c, False)
        return 0

    lax.fori_loop(0, i, body, 0)
    chunk(i, True)
    _softmax_finish(o_ref, l_s, acc_s, H, tq)


def _fox_attention(qk, v_t, misc, *, tq):
    B, S, _ = qk.shape
    return pl.pallas_call(
        _fox_kernel,
        grid=(B, S // tq),
        in_specs=[pl.BlockSpec((None, tq, FOX_W), lambda b, i: (b, i, 0)),
                  pl.BlockSpec((None, S, FOX_W), lambda b, i: (b, 0, 1)),
                  pl.BlockSpec((None, FOX_W, S), lambda b, i: (b, 0, 0)),
                  pl.BlockSpec((None, S, LANES), lambda b, i: (b, 0, 0))],
        out_specs=pl.BlockSpec((None, tq, FOX_W), lambda b, i: (b, i, 0)),
        out_shape=jax.ShapeDtypeStruct((B, S, FOX_W), MM_DTYPE),
        scratch_shapes=_softmax_scratch(FOX_HEADS, tq),
        compiler_params=pltpu.CompilerParams(
            dimension_semantics=("arbitrary", "arbitrary"), vmem_limit_bytes=VMEM_LIMIT),
        name="fox_attention",
    )(qk, qk, v_t, misc)


IDX_BITS = 13
SEARCH_MAX_ITERS = 256
F32_TINY = 1.1754943508222875e-38


def _dsa_kernel(qi_ref, ki_ref, m_ref, qd_ref, kd_ref, vt_ref, o_ref,
                sc_ref, jt_ref, m_s, l_s, acc_s, *, topk):
    i = pl.program_id(1)
    tq = qi_ref.shape[0]
    ch = tq
    n_ch = i + 1
    key_off = lax.broadcasted_iota(jnp.int32, (ch, tq), 0)
    q_off = lax.broadcasted_iota(jnp.int32, (ch, tq), 1)
    causal_diag = key_off <= q_off

    half_lane = lax.broadcasted_iota(jnp.int32, (tq, LANES), 1) // IDX_DIM
    q_heads = []
    for h in range(IDX_HEADS):
        pair = qi_ref[:, (h // 2) * LANES:(h // 2 + 1) * LANES]
        q_heads.append(jnp.where(half_lane == h % 2, pair, jnp.zeros_like(pair)))
    w_t = m_ref[...].T

    def score_chunk(c, diag, carry):
        hi8, lo8 = carry
        start = pl.multiple_of(c * ch, ch)
        kc = ki_ref[pl.ds(start, ch), :]
        acc = jnp.zeros((ch, tq), F32)
        for h in range(IDX_HEADS):
            acc = acc + jnp.maximum(_nt_dot(kc, q_heads[h]), 0.0) * w_t[h:h + 1, :]
        lo_src = acc
        if diag:
            lo_src = jnp.where(causal_diag, acc, jnp.inf)
            acc = jnp.where(causal_diag, acc, -jnp.inf)
        sc_ref[pl.ds(start, ch), :] = acc
        return (jnp.maximum(hi8, _sublane_fold(acc, jnp.maximum)),
                jnp.minimum(lo8, _sublane_fold(lo_src, jnp.minimum)))

    carry = (jnp.full((SUBLANES, tq), -jnp.inf, F32), jnp.full((SUBLANES, tq), jnp.inf, F32))
    carry = lax.fori_loop(0, i, lambda c, cr: score_chunk(c, False, cr), carry)
    hi8, lo8 = score_chunk(i, True, carry)
    row_max = jnp.max(hi8, axis=0, keepdims=True)
    row_min = jnp.min(lo8, axis=0, keepdims=True)

    def count(pred):
        def body(c, cnt):
            start = pl.multiple_of(c * ch, ch)
            hit = pred(sc_ref[pl.ds(start, ch), :], start)
            return cnt + _sublane_fold(jnp.where(hit, 1.0, 0.0), jnp.add)
        part = lax.fori_loop(0, n_ch, body, jnp.zeros((SUBLANES, tq), F32))
        return jnp.sum(part, axis=0, keepdims=True)

    kf = jnp.float32(topk)
    n_causal = (i * tq + 1 + lax.broadcasted_iota(jnp.int32, (1, tq), 1)).astype(F32)
    few = n_causal < kf

    def active(lo, hi, c_lo):
        mid = 0.5 * lo + 0.5 * hi
        return (c_lo > kf) & (mid > lo) & (mid < hi)

    def any_active(state):
        lo, hi, c_lo, it = state[0], state[1], state[2], state[-1]
        busy = jnp.max(jnp.where(active(lo, hi, c_lo), 1.0, 0.0))
        return (busy > 0.5) & (it < SEARCH_MAX_ITERS)

    log_k = jnp.log(kf - 0.5)

    def narrow(state):
        lo, hi, c_lo, c_hi, w_lo, w_hi, last, it = state
        act = active(lo, hi, c_lo)
        mid = 0.5 * lo + 0.5 * hi
        f_lo = (jnp.log(c_lo) - log_k) * w_lo
        f_hi = (log_k - jnp.log(jnp.maximum(c_hi, 0.25))) * w_hi
        guess = lo + (hi - lo) * (f_lo / (f_lo + f_hi))
        cand = jnp.where((guess > lo) & (guess < hi), guess, mid)
        cand = jnp.where((lo < 0.0) & (hi > 0.0), 0.0, cand)
        cand = jnp.where((lo == 0.0) & (hi > F32_TINY), F32_TINY, cand)
        cb = jnp.broadcast_to(cand, (ch, tq))
        cnt = count(lambda s, _: s >= cb)
        up = act & (cnt >= kf)
        dn = act & (cnt < kf)
        w_lo = jnp.where(up, 1.0, jnp.where(dn, jnp.where(last < 0.0, 0.5 * w_lo, 1.0), w_lo))
        w_hi = jnp.where(dn, 1.0, jnp.where(up, jnp.where(last > 0.0, 0.5 * w_hi, 1.0), w_hi))
        last = jnp.where(up, 1.0, jnp.where(dn, -1.0, last))
        return (jnp.where(up, cand, lo), jnp.where(dn, cand, hi),
                jnp.where(up, cnt, c_lo), jnp.where(dn, cnt, c_hi), w_lo, w_hi, last, it + 1)

    above_max = row_max + (row_max - row_min) + 1.0
    ones = jnp.ones((1, tq), F32)
    start_state = (row_min, above_max, jnp.where(few, kf, n_causal), jnp.zeros((1, tq), F32),
                   ones, ones, jnp.zeros((1, tq), F32), jnp.int32(0))
    lo, _, c_lo, c_hi = lax.while_loop(any_active, narrow, start_state)[:4]

    t = jnp.where(few, -jnp.inf, lo)
    tb = jnp.broadcast_to(t, (ch, tq))

    need = kf - c_hi
    jt_ref[...] = jnp.full(jt_ref.shape, 2 ** IDX_BITS - 1, jnp.int32)

    @pl.when(jnp.max(c_lo) > kf)
    def _():
        def idx_step(n, jt):
            cand = jt | (jnp.int32(1) << (IDX_BITS - 1 - n))
            cb = jnp.broadcast_to(cand, (ch, tq))
            below = count(lambda s, start: (s == tb) & (start + key_off < cb))
            return jnp.where(below <= need, cand, jt)
        jt = lax.fori_loop(0, IDX_BITS, idx_step, jnp.zeros((1, tq), jnp.int32))
        jt = jnp.where(c_lo > kf, jt, 2 ** IDX_BITS - 1)
        jt_ref[...] = jnp.broadcast_to(jt, jt_ref.shape)

    jb = jnp.broadcast_to(jt_ref[0:1, :], (ch, tq))

    _softmax_init(m_s, l_s, acc_s)
    q_all = jnp.concatenate([qd_ref[:, h * HEAD_DIM:(h + 1) * HEAD_DIM]
                             for h in range(DSA_HEADS)], axis=0)

    def attn_chunk(c, diag):
        start = pl.multiple_of(c * ch, ch)
        s_idx = sc_ref[pl.ds(start, ch), :]
        sel = (s_idx > tb) | ((s_idx == tb) & (start + key_off < jb))
        if diag:
            sel = sel & causal_diag
        s = _nt_dot(kd_ref[pl.ds(start, ch), :], q_all)
        s = jnp.concatenate([jnp.where(sel, s[:, h * tq:(h + 1) * tq], NEG_BIG)
                             for h in range(DSA_HEADS)], axis=1)
        _softmax_step(s, lambda p: _dot(vt_ref[:, pl.ds(start, ch)], p), m_s, l_s, acc_s)

    def attn_body(c, _):
        attn_chunk(c, False)
        return 0

    lax.fori_loop(0, i, attn_body, 0)
    attn_chunk(i, True)
    _softmax_finish(o_ref, l_s, acc_s, DSA_HEADS, tq)


def _dsa_attention(qk_i, misc, qk_d, v_t, *, tq, topk):
    B, S, _ = qk_d.shape
    blk = lambda w, col: pl.BlockSpec((None, tq, w), lambda b, i: (b, i, col))
    res = lambda col: pl.BlockSpec((None, S, LANES), lambda b, i: (b, 0, col))
    return pl.pallas_call(
        functools.partial(_dsa_kernel, topk=topk),
        grid=(B, S // tq),
        in_specs=[blk(IDX_W, 0), res(IDX_W // LANES), blk(LANES, 0),
                  blk(DSA_W, 0), res(DSA_W // LANES),
                  pl.BlockSpec((None, HEAD_DIM, S), lambda b, i: (b, 0, 0))],
        out_specs=blk(DSA_W, 0),
        out_shape=jax.ShapeDtypeStruct((B, S, DSA_W), MM_DTYPE),
        scratch_shapes=[pltpu.VMEM((S, tq), F32),
                        pltpu.VMEM((SUBLANES, tq), jnp.int32)]
                       + _softmax_scratch(DSA_HEADS, tq),
        compiler_params=pltpu.CompilerParams(
            dimension_semantics=("arbitrary", "arbitrary"), vmem_limit_bytes=VMEM_LIMIT),
        name="dsa_attention",
    )(qk_i, qk_i, misc, qk_d, qk_d, v_t)


def _merge_kernel(of_ref, od_ref, gate_ref, x_ref, wbf_ref, wbd_ref, wo_ref, g_ref, h_ref):
    a = _dot(of_ref[...], wbf_ref[...])
    b = _dot(od_ref[...], wbd_ref[...])
    mixed = (gate_ref[:, 0:D_MODEL].astype(F32) * a
             + gate_ref[:, D_MODEL:2 * D_MODEL].astype(F32) * b)
    y = _dot(mixed.astype(MM_DTYPE), wo_ref[...])
    h_ref[...] = x_ref[...] + _rms(y, g_ref[...])


def _merge_stage(o_f, o_d, gates, x, wbf, wbd, wo, g_post, *, tm):
    N, D = x.shape
    row = lambda w: pl.BlockSpec((tm, w), lambda t: (t, 0))
    full = lambda a: pl.BlockSpec(a.shape, lambda t: (0,) * a.ndim)
    return pl.pallas_call(
        _merge_kernel,
        grid=(N // tm,),
        in_specs=[row(FOX_W), row(DSA_W), row(2 * D), row(D),
                  full(wbf), full(wbd), full(wo), full(g_post)],
        out_specs=row(D),
        out_shape=jax.ShapeDtypeStruct((N, D), F32),
        compiler_params=pltpu.CompilerParams(
            dimension_semantics=("arbitrary",), vmem_limit_bytes=VMEM_LIMIT),
        name="merge_stage",
    )(o_f, o_d, gates, x, wbf, wbd, wo, g_post)


def _ffn_kernel(h_ref, gpre_ref, wg_ref, wu_ref, wd_ref, gpost_ref, o_ref, v_ref, acc_ref):
    j = pl.program_id(1)

    @pl.when(j == 0)
    def _():
        v_ref[...] = _rms(h_ref[...], gpre_ref[...]).astype(v_ref.dtype)
        acc_ref[...] = jnp.zeros_like(acc_ref)

    v = v_ref[...]
    g = _dot(v, wg_ref[...])
    a = (g * _sigmoid(g) * _dot(v, wu_ref[...])).astype(MM_DTYPE)
    acc_ref[...] += _dot(a, wd_ref[...])

    @pl.when(j == pl.num_programs(1) - 1)
    def _():
        o_ref[...] = h_ref[...] + _rms(acc_ref[...], gpost_ref[...])


def _ffn_stage(h, g_pre, wg, wu, wd, g_post, *, tm, tf):
    N, D = h.shape
    F = wg.shape[1]
    full = lambda a: pl.BlockSpec(a.shape, lambda t, j: (0,) * a.ndim)
    return pl.pallas_call(
        _ffn_kernel,
        grid=(N // tm, F // tf),
        in_specs=[pl.BlockSpec((tm, D), lambda t, j: (t, 0)), full(g_pre),
                  pl.BlockSpec((D, tf), lambda t, j: (0, j)),
                  pl.BlockSpec((D, tf), lambda t, j: (0, j)),
                  pl.BlockSpec((tf, D), lambda t, j: (j, 0)), full(g_post)],
        out_specs=pl.BlockSpec((tm, D), lambda t, j: (t, 0)),
        out_shape=jax.ShapeDtypeStruct((N, D), F32),
        scratch_shapes=[pltpu.VMEM((tm, D), MM_DTYPE), pltpu.VMEM((tm, D), F32)],
        compiler_params=pltpu.CompilerParams(
            dimension_semantics=("arbitrary", "arbitrary"), vmem_limit_bytes=VMEM_LIMIT),
        name="ffn_stage",
    )(h, g_pre, wg, wu, wd, g_post)


def _rope_tables(S, rot, period):
    half = rot // 2
    inv_freq = jnp.float32(ROPE_THETA) ** (-jnp.arange(half, dtype=F32) * 2.0 / rot)
    ang = jnp.arange(S).astype(F32)[:, None] * inv_freq[None, :]
    cos, sin = jnp.cos(ang), jnp.sin(ang)
    one = jnp.ones((S, period - rot), F32)
    zero = jnp.zeros((S, period - rot), F32)
    zh = jnp.zeros((S, half), F32)
    reps = LANES // period
    cos_t = jnp.tile(jnp.concatenate([cos, cos, one], axis=1), (1, reps))
    sin_a = jnp.tile(jnp.concatenate([-sin, zh, zero], axis=1), (1, reps))
    sin_b = jnp.tile(jnp.concatenate([zh, sin, zero], axis=1), (1, reps))
    return cos_t, sin_a, sin_b


def _split_w_in(w):
    parts, off = [], 0
    for width in IN_WIDTHS:
        parts.append(w[:, off:off + width])
        off += width
    return parts


def _layer(h, p, tabs, *, tm_in, tq_fox, tq_dsa, tm_merge, tm_ffn, tf):
    B, S, D = h.shape
    (w_qf, w_kf, w_vf, w_fl, w_qd, w_kd, w_vd, w_qi, w_ki, w_wi, w_gf, w_gd) = _split_w_in(p["w_in"])
    cast = lambda a: a.astype(MM_DTYPE)
    pad_m = jnp.zeros((D, LANES - IDX_HEADS - FOX_HEADS), F32)
    wf = cast(jnp.concatenate([w_qf, w_kf, w_vf], axis=1))
    wd = cast(jnp.concatenate([w_qd, w_kd, w_vd], axis=1))
    wi = cast(jnp.concatenate([w_qi, w_ki, w_ki], axis=1))
    wm = cast(jnp.concatenate([w_wi, w_fl, pad_m], axis=1))
    wg = cast(jnp.concatenate([w_gf, w_gd], axis=1))
    bm = jnp.zeros((1, LANES), F32).at[0, MISC_F0:MISC_F0 + FOX_HEADS].set(p["b_forget"].astype(F32))
    bg = p["b_gate"].astype(F32).reshape(1, 2 * D)
    row = lambda a: a.astype(F32).reshape(1, D)

    qk_f, vt_f, qk_d, vt_d, qk_i, misc, gates = _input_stage(
        h, row(p["norm_mix_pre"]), wf, wd, wi, wm, wg, bm, bg, tabs, tm=tm_in)

    o_f = _fox_attention(qk_f, vt_f, misc, tq=tq_fox)
    o_d = _dsa_attention(qk_i, misc, qk_d, vt_d, tq=tq_dsa, topk=min(TOPK_MAX, S // 4))

    N = B * S
    h1 = _merge_stage(o_f.reshape(N, FOX_W), o_d.reshape(N, DSA_W), gates.reshape(N, 2 * D),
                      h.reshape(N, D), cast(p["w_branch_fox"]), cast(p["w_branch_dsa"]),
                      cast(p["w_out"]), row(p["norm_mix_post"]), tm=tm_merge)
    h2 = _ffn_stage(h1, row(p["norm_ffn_pre"]), cast(p["w_ffn_gate"]), cast(p["w_ffn_up"]),
                    cast(p["w_ffn_down"]), row(p["norm_ffn_post"]), tm=tm_ffn, tf=tf)
    return h2.reshape(B, S, D)


def kernel(x, norm_mix_pre, w_in, b_forget, b_gate, w_branch_fox, w_branch_dsa, w_out,
           norm_mix_post, norm_ffn_pre, w_ffn_gate, w_ffn_up, w_ffn_down, norm_ffn_post):
    B, S, D = x.shape
    params = dict(norm_mix_pre=norm_mix_pre, w_in=w_in, b_forget=b_forget, b_gate=b_gate,
                  w_branch_fox=w_branch_fox, w_branch_dsa=w_branch_dsa, w_out=w_out,
                  norm_mix_post=norm_mix_post, norm_ffn_pre=norm_ffn_pre, w_ffn_gate=w_ffn_gate,
                  w_ffn_up=w_ffn_up, w_ffn_down=w_ffn_down, norm_ffn_post=norm_ffn_post)
    tabs = (_rope_tables(S, HEAD_DIM // ROT_FRAC_DEN, HEAD_DIM)
            + _rope_tables(S, IDX_DIM // ROT_FRAC_DEN, IDX_DIM))
    tiles = dict(tm_in=min(512, S), tq_fox=256, tq_dsa=256,
                 tm_merge=min(512, B * S), tm_ffn=min(1024, B * S), tf=D_FF // 2)
    h = x
    for l in range(w_in.shape[0]):
        h = _layer(h, {k: v[l] for k, v in params.items()}, tabs, **tiles)
    return h
```

```python
import functools

import jax
import jax.numpy as jnp
from jax import lax
from jax.experimental import pallas as pl
from jax.experimental.pallas import tpu as pltpu

D_MODEL = 1024
HEAD_DIM = 128
FOX_HEADS = 4
DSA_HEADS = 4
IDX_HEADS = 8
IDX_DIM = 64
ROT_FRAC_DEN = 4
ROPE_THETA = 500000.0
TOPK_MAX = 256
D_FF = 2816
RMS_EPS = 1e-6
FOX_W = FOX_HEADS * HEAD_DIM
DSA_W = DSA_HEADS * HEAD_DIM
IDX_W = IDX_HEADS * IDX_DIM
IN_WIDTHS = (FOX_W, FOX_W, FOX_W, FOX_HEADS, DSA_W, HEAD_DIM, HEAD_DIM,
             IDX_W, IDX_DIM, IDX_HEADS, D_MODEL, D_MODEL)

LANES = 128
SUBLANES = 8
VMEM_LIMIT = 56 * 1024 * 1024
MM_DTYPE = jnp.bfloat16
NEG_BIG = -1e30
LOG2E = 1.4426950408889634
Q_SCALE = HEAD_DIM ** -0.5 * LOG2E

F32 = jnp.float32


def _nt_dot(a, b):
    return lax.dot_general(a, b, (((1,), (1,)), ((), ())), preferred_element_type=F32)


def _dot(a, b):
    return jnp.dot(a, b, preferred_element_type=F32)


def _rms(x, g):
    return x * lax.rsqrt(jnp.mean(x * x, axis=-1, keepdims=True) + RMS_EPS) * g


def _sigmoid(z):
    return 1.0 / (1.0 + jnp.exp(-z))


def _sublane_fold(x, op):
    acc = x[0:SUBLANES, :]
    for j in range(1, x.shape[0] // SUBLANES):
        acc = op(acc, x[j * SUBLANES:(j + 1) * SUBLANES, :])
    return acc


MISC_F0 = IDX_HEADS


def _rope(seg, cos, sin_a, sin_b, half):
    return (seg * cos
            + pltpu.roll(seg, LANES - half, axis=1) * sin_a
            + pltpu.roll(seg, half, axis=1) * sin_b)


def _input_kernel(x_ref, g_ref, wf_ref, wd_ref, wi_ref, wm_ref, wg_ref, bm_ref, bg_ref,
                  cd_ref, sad_ref, sbd_ref, ci_ref, sai_ref, sbi_ref,
                  f_ref, vtf_ref, d_ref, vtd_ref, i_ref, m_ref, gate_ref, carry_ref, *, idx_scale):
    tm = x_ref.shape[0]
    u = _rms(x_ref[...], g_ref[...]).astype(MM_DTYPE)

    yf = _dot(u, wf_ref[...])
    f_ref[:, 0:FOX_W] = (yf[:, 0:FOX_W] * Q_SCALE).astype(f_ref.dtype)
    f_ref[:, FOX_W:2 * FOX_W] = yf[:, FOX_W:2 * FOX_W].astype(f_ref.dtype)
    for h in range(FOX_HEADS):
        v = yf[:, 2 * FOX_W + h * HEAD_DIM:2 * FOX_W + (h + 1) * HEAD_DIM]
        vtf_ref[h * HEAD_DIM:(h + 1) * HEAD_DIM, :] = v.T.astype(vtf_ref.dtype)

    yd = _dot(u, wd_ref[...])
    cd, sad, sbd = cd_ref[...], sad_ref[...], sbd_ref[...]
    for j in range(DSA_HEADS + 1):
        seg = _rope(yd[:, j * LANES:(j + 1) * LANES], cd, sad, sbd, HEAD_DIM // ROT_FRAC_DEN // 2)
        if j < DSA_HEADS:
            seg = seg * Q_SCALE
        d_ref[:, j * LANES:(j + 1) * LANES] = seg.astype(d_ref.dtype)
    vtd_ref[...] = yd[:, (DSA_HEADS + 1) * LANES:].T.astype(vtd_ref.dtype)

    yi = _dot(u, wi_ref[...])
    ci, sai, sbi = ci_ref[...], sai_ref[...], sbi_ref[...]
    for j in range(yi.shape[1] // LANES):
        seg = _rope(yi[:, j * LANES:(j + 1) * LANES], ci, sai, sbi, IDX_DIM // ROT_FRAC_DEN // 2)
        i_ref[:, j * LANES:(j + 1) * LANES] = seg.astype(i_ref.dtype)

    gate_ref[...] = _sigmoid(_dot(u, wg_ref[...]) + bg_ref[...]).astype(gate_ref.dtype)

    ym = _dot(u, wm_ref[...])
    col = lax.broadcasted_iota(jnp.int32, ym.shape, 1)
    row = lax.broadcasted_iota(jnp.int32, ym.shape, 0)
    is_f = (col >= MISC_F0) & (col < MISC_F0 + FOX_HEADS)
    z = ym + bm_ref[...]
    log_f = jnp.where(is_f, jnp.minimum(z, 0.0) - jnp.log1p(jnp.exp(-jnp.abs(z))), 0.0)

    @pl.when(pl.program_id(1) == 0)
    def _():
        carry_ref[...] = jnp.zeros_like(carry_ref)

    c = log_f
    shift = 1
    while shift < tm:
        c = c + jnp.where(row >= shift, pltpu.roll(c, shift, axis=0), 0.0)
        shift *= 2
    c = c + carry_ref[0:1, :]
    carry_ref[0:1, :] = c[tm - 1:tm, :]
    m_ref[...] = jnp.where(is_f, c * LOG2E, ym * idx_scale)


def _input_stage(x, gain, wf, wd, wi, wm, wg, bm, bg, tabs, *, tm):
    B, S, D = x.shape
    idx_scale = (IDX_DIM ** -0.5) * (IDX_HEADS ** -0.5)
    full = lambda a: pl.BlockSpec(a.shape, lambda b, t: (0,) * a.ndim)
    tab = pl.BlockSpec((tm, LANES), lambda b, t: (t, 0))
    rows = lambda w: pl.BlockSpec((None, tm, w), lambda b, t: (b, t, 0))
    cols = lambda h: pl.BlockSpec((None, h, tm), lambda b, t: (b, 0, t))
    qk_d_w = DSA_W + HEAD_DIM
    return pl.pallas_call(
        functools.partial(_input_kernel, idx_scale=idx_scale),
        grid=(B, S // tm),
        in_specs=[rows(D), full(gain), full(wf), full(wd), full(wi), full(wm), full(wg),
                  full(bm), full(bg)] + [tab] * 6,
        out_specs=[rows(2 * FOX_W), cols(FOX_W), rows(qk_d_w), cols(HEAD_DIM),
                   rows(wi.shape[1]), rows(wm.shape[1]), rows(wg.shape[1])],
        out_shape=[jax.ShapeDtypeStruct((B, S, 2 * FOX_W), MM_DTYPE),
                   jax.ShapeDtypeStruct((B, FOX_W, S), MM_DTYPE),
                   jax.ShapeDtypeStruct((B, S, qk_d_w), MM_DTYPE),
                   jax.ShapeDtypeStruct((B, HEAD_DIM, S), MM_DTYPE),
                   jax.ShapeDtypeStruct((B, S, wi.shape[1]), MM_DTYPE),
                   jax.ShapeDtypeStruct((B, S, wm.shape[1]), F32),
                   jax.ShapeDtypeStruct((B, S, wg.shape[1]), MM_DTYPE)],
        scratch_shapes=[pltpu.VMEM((8, LANES), F32)],
        compiler_params=pltpu.CompilerParams(
            dimension_semantics=("arbitrary", "arbitrary"), vmem_limit_bytes=VMEM_LIMIT),
        name="input_stage",
    )(x, gain, wf, wd, wi, wm, wg, bm, bg, *tabs)


def _flash_attend(i, qk_fn, mask_fn, pv_fn, o_ref, scratch, heads, tq, side_fn=None):
    s_buf, p_buf, m_s, l_s, a_s, acc_s = scratch
    m_s[...] = jnp.full(m_s.shape, NEG_BIG, F32)
    l_s[...] = jnp.zeros(l_s.shape, F32)
    a_s[...] = jnp.ones(a_s.shape, F32)
    acc_s[...] = jnp.zeros(acc_s.shape, F32)
    p_buf[1] = jnp.zeros(p_buf.shape[1:], p_buf.dtype)
    s_buf[0] = qk_fn(0)

    def lagged_values(c, slot):
        acc_s[...] = a_s[0:1, :] * acc_s[...] + pv_fn(c, p_buf[slot])

    def stage(c, cur, last):
        lagged_values(jnp.maximum(c - 1, 0), 1 - cur)
        if not last:
            s_buf[1 - cur] = qk_fn(c + 1)
        if side_fn is not None:
            side_fn(c, last)
        s = mask_fn(c, s_buf[cur], last)
        m_old = m_s[0:1, :]
        m_new = jnp.maximum(m_old, jnp.max(s, axis=0, keepdims=True))
        alpha = jnp.exp2(m_old - m_new)
        p = jnp.exp2(s - m_new)
        l_s[0:1, :] = alpha * l_s[0:1, :] + jnp.sum(p, axis=0, keepdims=True)
        p_buf[cur] = p.astype(p_buf.dtype)
        a_s[0:1, :] = alpha
        m_s[0:1, :] = m_new

    def pair(j, _):
        stage(2 * j, 0, False)
        stage(2 * j + 1, 1, False)
        return 0

    lax.fori_loop(0, i // 2, pair, 0)

    @pl.when(i % 2 == 0)
    def _():
        stage(i, 0, True)
        lagged_values(i, 0)

    @pl.when(i % 2 == 1)
    def _():
        stage(i - 1, 0, False)
        stage(i, 1, True)
        lagged_values(i, 1)

    for h in range(heads):
        cols = slice(h * tq, (h + 1) * tq)
        o_t = acc_s[:, cols] / l_s[0:1, cols]
        o_ref[:, h * HEAD_DIM:(h + 1) * HEAD_DIM] = o_t.T.astype(o_ref.dtype)


def _flash_scratch(heads, tq):
    w = heads * tq
    return [pltpu.VMEM((2, tq, w), F32),
            pltpu.VMEM((2, tq, w), MM_DTYPE),
            pltpu.VMEM((SUBLANES, w), F32),
            pltpu.VMEM((SUBLANES, w), F32),
            pltpu.VMEM((SUBLANES, w), F32),
            pltpu.VMEM((HEAD_DIM, w), F32)]


def _fox_fns(i, q_ref, k_ref, vt_ref, m_ref, tq):
    ch = tq
    H = FOX_HEADS
    causal_diag = (lax.broadcasted_iota(jnp.int32, (ch, H * tq), 0)
                   <= lax.broadcasted_iota(jnp.int32, (ch, H * tq), 1) % tq)

    def qk(c):
        start = pl.multiple_of(c * ch, ch)
        parts = []
        for h in range(H):
            hd = slice(h * HEAD_DIM, (h + 1) * HEAD_DIM)
            s = _nt_dot(k_ref[pl.ds(start, ch), hd], q_ref[:, hd])
            parts.append(s - m_ref[pl.ds(start, ch), MISC_F0 + h:MISC_F0 + h + 1])
        return jnp.concatenate(parts, axis=1)

    def mask(c, s, last):
        return jnp.where(causal_diag, s, NEG_BIG) if last else s

    def pv(c, p):
        start = pl.multiple_of(c * ch, ch)
        return jnp.concatenate(
            [_dot(vt_ref[h * HEAD_DIM:(h + 1) * HEAD_DIM, pl.ds(start, ch)],
                  p[:, h * tq:(h + 1) * tq]) for h in range(H)], axis=1)

    return qk, mask, pv


IDX_BITS = 13
SEARCH_MAX_ITERS = 256
F32_TINY = 1.1754943508222875e-38


def _mixers_kernel(qf_ref, kf_ref, vtf_ref, m_ref, qi_ref, ki_ref, qd_ref, kd_ref, vtd_ref,
                   of_ref, od_ref, sc_ref, ext_ref, *flash_scratch, topk):
    i = pl.program_id(1)
    tq = qi_ref.shape[0]
    ch = tq
    n_ch = i + 1
    key_off = lax.broadcasted_iota(jnp.int32, (ch, tq), 0)
    q_off = lax.broadcasted_iota(jnp.int32, (ch, tq), 1)
    causal_diag = key_off <= q_off

    half_lane = lax.broadcasted_iota(jnp.int32, (tq, LANES), 1) // IDX_DIM
    q_heads = []
    for h in range(IDX_HEADS):
        pair = qi_ref[:, (h // 2) * LANES:(h // 2 + 1) * LANES]
        q_heads.append(jnp.where(half_lane == h % 2, pair, jnp.zeros_like(pair)))
    w_t = m_ref[pl.ds(pl.multiple_of(i * tq, tq), tq), :].T
    ext_ref[0:SUBLANES, :] = jnp.full((SUBLANES, tq), -jnp.inf, F32)
    ext_ref[SUBLANES:2 * SUBLANES, :] = jnp.full((SUBLANES, tq), jnp.inf, F32)

    def score_chunk(c, diag):
        start = pl.multiple_of(c * ch, ch)
        kc = ki_ref[pl.ds(start, ch), :]
        acc = jnp.zeros((ch, tq), F32)
        for h in range(IDX_HEADS):
            acc = acc + jnp.maximum(_nt_dot(kc, q_heads[h]), 0.0) * w_t[h:h + 1, :]
        lo_src = acc
        if diag:
            lo_src = jnp.where(causal_diag, acc, jnp.inf)
            acc = jnp.where(causal_diag, acc, -jnp.inf)
        sc_ref[pl.ds(start, ch), :] = acc
        ext_ref[0:SUBLANES, :] = jnp.maximum(ext_ref[0:SUBLANES, :],
                                             _sublane_fold(acc, jnp.maximum))
        ext_ref[SUBLANES:2 * SUBLANES, :] = jnp.minimum(ext_ref[SUBLANES:2 * SUBLANES, :],
                                                        _sublane_fold(lo_src, jnp.minimum))

    fox_qk, fox_mask, fox_pv = _fox_fns(i, qf_ref, kf_ref, vtf_ref, m_ref, tq)
    _flash_attend(i, fox_qk, fox_mask, fox_pv, of_ref, flash_scratch, FOX_HEADS, tq,
                  side_fn=score_chunk)

    row_max = jnp.max(ext_ref[0:SUBLANES, :], axis=0, keepdims=True)
    row_min = jnp.min(ext_ref[SUBLANES:2 * SUBLANES, :], axis=0, keepdims=True)

    def count(pred):
        def body(c, cnt):
            start = pl.multiple_of(c * ch, ch)
            hit = pred(sc_ref[pl.ds(start, ch), :], start)
            return cnt + _sublane_fold(jnp.where(hit, 1.0, 0.0), jnp.add)
        part = lax.fori_loop(0, n_ch, body, jnp.zeros((SUBLANES, tq), F32))
        return jnp.sum(part, axis=0, keepdims=True)

    kf = jnp.float32(topk)
    n_causal = (i * tq + 1 + lax.broadcasted_iota(jnp.int32, (1, tq), 1)).astype(F32)
    few = n_causal < kf

    def active(lo, hi, c_lo):
        mid = 0.5 * lo + 0.5 * hi
        return (c_lo > kf) & (mid > lo) & (mid < hi)

    def any_active(state):
        lo, hi, c_lo, it = state[0], state[1], state[2], state[-1]
        busy = jnp.max(jnp.where(active(lo, hi, c_lo), 1.0, 0.0))
        return (busy > 0.5) & (it < SEARCH_MAX_ITERS)

    log_k = jnp.log(kf - 0.5)

    def narrow(state):
        lo, hi, c_lo, c_hi, w_lo, w_hi, last, it = state
        act = active(lo, hi, c_lo)
        mid = 0.5 * lo + 0.5 * hi
        f_lo = (jnp.log(c_lo) - log_k) * w_lo
        f_hi = (log_k - jnp.log(jnp.maximum(c_hi, 0.25))) * w_hi
        guess = lo + (hi - lo) * (f_lo / (f_lo + f_hi))
        cand = jnp.where((guess > lo) & (guess < hi), guess, mid)
        cand = jnp.where((lo < 0.0) & (hi > 0.0), 0.0, cand)
        cand = jnp.where((lo == 0.0) & (hi > F32_TINY), F32_TINY, cand)
        cb = jnp.broadcast_to(cand, (ch, tq))
        cnt = count(lambda s, _: s >= cb)
        up = act & (cnt >= kf)
        dn = act & (cnt < kf)
        w_lo = jnp.where(up, 1.0, jnp.where(dn, jnp.where(last < 0.0, 0.5 * w_lo, 1.0), w_lo))
        w_hi = jnp.where(dn, 1.0, jnp.where(up, jnp.where(last > 0.0, 0.5 * w_hi, 1.0), w_hi))
        last = jnp.where(up, 1.0, jnp.where(dn, -1.0, last))
        return (jnp.where(up, cand, lo), jnp.where(dn, cand, hi),
                jnp.where(up, cnt, c_lo), jnp.where(dn, cnt, c_hi), w_lo, w_hi, last, it + 1)

    above_max = row_max + (row_max - row_min) + 1.0
    ones = jnp.ones((1, tq), F32)
    start_state = (row_min, above_max, jnp.where(few, kf, n_causal), jnp.zeros((1, tq), F32),
                   ones, ones, jnp.zeros((1, tq), F32), jnp.int32(0))
    lo, _, c_lo, c_hi = lax.while_loop(any_active, narrow, start_state)[:4]

    t = jnp.where(few, -jnp.inf, lo)
    tb = jnp.broadcast_to(t, (ch, tq))

    need = kf - c_hi

    @pl.when(jnp.max(c_lo) > kf)
    def _():
        def idx_step(n, jt):
            cand = jt | (jnp.int32(1) << (IDX_BITS - 1 - n))
            cb = jnp.broadcast_to(cand, (ch, tq))
            below = count(lambda s, start: (s == tb) & (start + key_off < cb))
            return jnp.where(below <= need, cand, jt)
        jt = lax.fori_loop(0, IDX_BITS, idx_step, jnp.zeros((1, tq), jnp.int32))
        jt = jnp.where(c_lo > kf, jt, 2 ** IDX_BITS - 1)
        jb = jnp.broadcast_to(jt, (ch, tq))

        def drop_surplus(c, _):
            start = pl.multiple_of(c * ch, ch)
            s = sc_ref[pl.ds(start, ch), :]
            surplus = (s == tb) & (start + key_off >= jb)
            sc_ref[pl.ds(start, ch), :] = jnp.where(surplus, -jnp.inf, s)
            return 0
        lax.fori_loop(0, n_ch, drop_surplus, 0)

    q_all = jnp.concatenate([qd_ref[:, h * HEAD_DIM:(h + 1) * HEAD_DIM]
                             for h in range(DSA_HEADS)], axis=0)

    def qk(c):
        return _nt_dot(kd_ref[pl.ds(pl.multiple_of(c * ch, ch), ch), :], q_all)

    def mask(c, s, last):
        start = pl.multiple_of(c * ch, ch)
        sel = sc_ref[pl.ds(start, ch), :] >= tb
        if last:
            sel = sel & causal_diag
        return jnp.concatenate([jnp.where(sel, s[:, h * tq:(h + 1) * tq], NEG_BIG)
                                for h in range(DSA_HEADS)], axis=1)

    def pv(c, p):
        return _dot(vtd_ref[:, pl.ds(pl.multiple_of(c * ch, ch), ch)], p)

    _flash_attend(i, qk, mask, pv, od_ref, flash_scratch, DSA_HEADS, tq)


def _token_mixers(qk_f, vt_f, misc, qk_i, qk_d, vt_d, *, tq, topk):
    B, S, _ = qk_d.shape
    assert FOX_HEADS == DSA_HEADS and FOX_W == DSA_W
    blk = lambda w, col: pl.BlockSpec((None, tq, w), lambda b, i: (b, i, col))
    res = lambda w, col: pl.BlockSpec((None, S, w), lambda b, i: (b, 0, col))
    res_t = lambda h: pl.BlockSpec((None, h, S), lambda b, i: (b, 0, 0))
    return pl.pallas_call(
        functools.partial(_mixers_kernel, topk=topk),
        grid=(B, S // tq),
        in_specs=[blk(FOX_W, 0), res(FOX_W, 1), res_t(FOX_W), res(LANES, 0),
                  blk(IDX_W, 0), res(LANES, IDX_W // LANES),
                  blk(DSA_W, 0), res(LANES, DSA_W // LANES), res_t(HEAD_DIM)],
        out_specs=[blk(FOX_W, 0), blk(DSA_W, 0)],
        out_shape=[jax.ShapeDtypeStruct((B, S, FOX_W), MM_DTYPE),
                   jax.ShapeDtypeStruct((B, S, DSA_W), MM_DTYPE)],
        scratch_shapes=[pltpu.VMEM((S, tq), F32),
                        pltpu.VMEM((2 * SUBLANES, tq), F32)]
                       + _flash_scratch(DSA_HEADS, tq),
        compiler_params=pltpu.CompilerParams(
            dimension_semantics=("arbitrary", "arbitrary"), vmem_limit_bytes=VMEM_LIMIT),
        name="token_mixers",
    )(qk_f, qk_f, vt_f, misc, qk_i, qk_i, qk_d, qk_d, vt_d)


def _merge_kernel(of_ref, od_ref, gate_ref, x_ref, wbf_ref, wbd_ref, wo_ref, g_ref, h_ref):
    a = _dot(of_ref[...], wbf_ref[...])
    b = _dot(od_ref[...], wbd_ref[...])
    mixed = (gate_ref[:, 0:D_MODEL].astype(F32) * a
             + gate_ref[:, D_MODEL:2 * D_MODEL].astype(F32) * b)
    y = _dot(mixed.astype(MM_DTYPE), wo_ref[...])
    h_ref[...] = x_ref[...] + _rms(y, g_ref[...])


def _merge_stage(o_f, o_d, gates, x, wbf, wbd, wo, g_post, *, tm):
    N, D = x.shape
    row = lambda w: pl.BlockSpec((tm, w), lambda t: (t, 0))
    full = lambda a: pl.BlockSpec(a.shape, lambda t: (0,) * a.ndim)
    return pl.pallas_call(
        _merge_kernel,
        grid=(N // tm,),
        in_specs=[row(FOX_W), row(DSA_W), row(2 * D), row(D),
                  full(wbf), full(wbd), full(wo), full(g_post)],
        out_specs=row(D),
        out_shape=jax.ShapeDtypeStruct((N, D), F32),
        compiler_params=pltpu.CompilerParams(
            dimension_semantics=("arbitrary",), vmem_limit_bytes=VMEM_LIMIT),
        name="merge_stage",
    )(o_f, o_d, gates, x, wbf, wbd, wo, g_post)


def _ffn_kernel(h_ref, gpre_ref, wg_ref, wu_ref, wd_ref, gpost_ref, o_ref, v_ref, acc_ref):
    j = pl.program_id(1)

    @pl.when(j == 0)
    def _():
        v_ref[...] = _rms(h_ref[...], gpre_ref[...]).astype(v_ref.dtype)
        acc_ref[...] = jnp.zeros_like(acc_ref)

    v = v_ref[...]
    g = _dot(v, wg_ref[...])
    a = (g * _sigmoid(g) * _dot(v, wu_ref[...])).astype(MM_DTYPE)
    acc_ref[...] += _dot(a, wd_ref[...])

    @pl.when(j == pl.num_programs(1) - 1)
    def _():
        o_ref[...] = h_ref[...] + _rms(acc_ref[...], gpost_ref[...])


def _ffn_stage(h, g_pre, wg, wu, wd, g_post, *, tm, tf):
    N, D = h.shape
    F = wg.shape[1]
    full = lambda a: pl.BlockSpec(a.shape, lambda t, j: (0,) * a.ndim)
    return pl.pallas_call(
        _ffn_kernel,
        grid=(N // tm, F // tf),
        in_specs=[pl.BlockSpec((tm, D), lambda t, j: (t, 0)), full(g_pre),
                  pl.BlockSpec((D, tf), lambda t, j: (0, j)),
                  pl.BlockSpec((D, tf), lambda t, j: (0, j)),
                  pl.BlockSpec((tf, D), lambda t, j: (j, 0)), full(g_post)],
        out_specs=pl.BlockSpec((tm, D), lambda t, j: (t, 0)),
        out_shape=jax.ShapeDtypeStruct((N, D), F32),
        scratch_shapes=[pltpu.VMEM((tm, D), MM_DTYPE), pltpu.VMEM((tm, D), F32)],
        compiler_params=pltpu.CompilerParams(
            dimension_semantics=("arbitrary", "arbitrary"), vmem_limit_bytes=VMEM_LIMIT),
        name="ffn_stage",
    )(h, g_pre, wg, wu, wd, g_post)


def _rope_tables(S, rot, period):
    half = rot // 2
    inv_freq = jnp.float32(ROPE_THETA) ** (-jnp.arange(half, dtype=F32) * 2.0 / rot)
    ang = jnp.arange(S).astype(F32)[:, None] * inv_freq[None, :]
    cos, sin = jnp.cos(ang), jnp.sin(ang)
    one = jnp.ones((S, period - rot), F32)
    zero = jnp.zeros((S, period - rot), F32)
    zh = jnp.zeros((S, half), F32)
    reps = LANES // period
    cos_t = jnp.tile(jnp.concatenate([cos, cos, one], axis=1), (1, reps))
    sin_a = jnp.tile(jnp.concatenate([-sin, zh, zero], axis=1), (1, reps))
    sin_b = jnp.tile(jnp.concatenate([zh, sin, zero], axis=1), (1, reps))
    return cos_t, sin_a, sin_b


def _split_w_in(w):
    parts, off = [], 0
    for width in IN_WIDTHS:
        parts.append(w[:, off:off + width])
        off += width
    return parts


def _layer(h, p, tabs, *, tm_in, tq_mix, tm_merge, tm_ffn, tf):
    B, S, D = h.shape
    (w_qf, w_kf, w_vf, w_fl, w_qd, w_kd, w_vd, w_qi, w_ki, w_wi, w_gf, w_gd) = _split_w_in(p["w_in"])
    cast = lambda a: a.astype(MM_DTYPE)
    pad_m = jnp.zeros((D, LANES - IDX_HEADS - FOX_HEADS), F32)
    wf = cast(jnp.concatenate([w_qf, w_kf, w_vf], axis=1))
    wd = cast(jnp.concatenate([w_qd, w_kd, w_vd], axis=1))
    wi = cast(jnp.concatenate([w_qi, w_ki, w_ki], axis=1))
    wm = cast(jnp.concatenate([w_wi, w_fl, pad_m], axis=1))
    wg = cast(jnp.concatenate([w_gf, w_gd], axis=1))
    bm = jnp.zeros((1, LANES), F32).at[0, MISC_F0:MISC_F0 + FOX_HEADS].set(p["b_forget"].astype(F32))
    bg = p["b_gate"].astype(F32).reshape(1, 2 * D)
    row = lambda a: a.astype(F32).reshape(1, D)

    qk_f, vt_f, qk_d, vt_d, qk_i, misc, gates = _input_stage(
        h, row(p["norm_mix_pre"]), wf, wd, wi, wm, wg, bm, bg, tabs, tm=tm_in)

    o_f, o_d = _token_mixers(qk_f, vt_f, misc, qk_i, qk_d, vt_d, tq=tq_mix,
                             topk=min(TOPK_MAX, S // 4))

    N = B * S
    h1 = _merge_stage(o_f.reshape(N, FOX_W), o_d.reshape(N, DSA_W), gates.reshape(N, 2 * D),
                      h.reshape(N, D), cast(p["w_branch_fox"]), cast(p["w_branch_dsa"]),
                      cast(p["w_out"]), row(p["norm_mix_post"]), tm=tm_merge)
    h2 = _ffn_stage(h1, row(p["norm_ffn_pre"]), cast(p["w_ffn_gate"]), cast(p["w_ffn_up"]),
                    cast(p["w_ffn_down"]), row(p["norm_ffn_post"]), tm=tm_ffn, tf=tf)
    return h2.reshape(B, S, D)


def kernel(x, norm_mix_pre, w_in, b_forget, b_gate, w_branch_fox, w_branch_dsa, w_out,
           norm_mix_post, norm_ffn_pre, w_ffn_gate, w_ffn_up, w_ffn_down, norm_ffn_post):
    B, S, D = x.shape
    params = dict(norm_mix_pre=norm_mix_pre, w_in=w_in, b_forget=b_forget, b_gate=b_gate,
                  w_branch_fox=w_branch_fox, w_branch_dsa=w_branch_dsa, w_out=w_out,
                  norm_mix_post=norm_mix_post, norm_ffn_pre=norm_ffn_pre, w_ffn_gate=w_ffn_gate,
                  w_ffn_up=w_ffn_up, w_ffn_down=w_ffn_down, norm_ffn_post=norm_ffn_post)
    tabs = (_rope_tables(S, HEAD_DIM // ROT_FRAC_DEN, HEAD_DIM)
            + _rope_tables(S, IDX_DIM // ROT_FRAC_DEN, IDX_DIM))
    tiles = dict(tm_in=min(512, S), tq_mix=256,
                 tm_merge=min(512, B * S), tm_ffn=min(1024, B * S), tf=D_FF // 2)
    h = x
    for l in range(w_in.shape[0]):
        h = _layer(h, {k: v[l] for k, v in params.items()}, tabs, **tiles)
    return h
```

```python
import functools

import jax
import jax.numpy as jnp
from jax import lax
from jax.experimental import pallas as pl
from jax.experimental.pallas import tpu as pltpu

D_MODEL = 1024
HEAD_DIM = 128
FOX_HEADS = 4
DSA_HEADS = 4
IDX_HEADS = 8
IDX_DIM = 64
ROT_FRAC_DEN = 4
ROPE_THETA = 500000.0
TOPK_MAX = 256
D_FF = 2816
RMS_EPS = 1e-6
FOX_W = FOX_HEADS * HEAD_DIM
DSA_W = DSA_HEADS * HEAD_DIM
IDX_W = IDX_HEADS * IDX_DIM
IN_WIDTHS = (FOX_W, FOX_W, FOX_W, FOX_HEADS, DSA_W, HEAD_DIM, HEAD_DIM,
             IDX_W, IDX_DIM, IDX_HEADS, D_MODEL, D_MODEL)

LANES = 128
SUBLANES = 8
VMEM_LIMIT = 56 * 1024 * 1024
MM_DTYPE = jnp.bfloat16
NEG_BIG = -1e30
LOG2E = 1.4426950408889634
Q_SCALE = HEAD_DIM ** -0.5 * LOG2E

F32 = jnp.float32


def _nt_dot(a, b):
    return lax.dot_general(a, b, (((1,), (1,)), ((), ())), preferred_element_type=F32)


def _dot(a, b):
    return jnp.dot(a, b, preferred_element_type=F32)


def _rms(x, g):
    return x * lax.rsqrt(jnp.mean(x * x, axis=-1, keepdims=True) + RMS_EPS) * g


def _sigmoid(z):
    return 1.0 / (1.0 + jnp.exp(-z))


def _sublane_fold(x, op):
    acc = x[0:SUBLANES, :]
    for j in range(1, x.shape[0] // SUBLANES):
        acc = op(acc, x[j * SUBLANES:(j + 1) * SUBLANES, :])
    return acc


MISC_F0 = IDX_HEADS


def _rope(seg, cos, sin_a, sin_b, half):
    return (seg * cos
            + pltpu.roll(seg, LANES - half, axis=1) * sin_a
            + pltpu.roll(seg, half, axis=1) * sin_b)


def _input_kernel(x_ref, g_ref, wf_ref, wd_ref, wi_ref, wm_ref, wg_ref, bm_ref, bg_ref,
                  cd_ref, sad_ref, sbd_ref, ci_ref, sai_ref, sbi_ref,
                  f_ref, vtf_ref, d_ref, vtd_ref, i_ref, m_ref, gate_ref, carry_ref, *, idx_scale):
    tm = x_ref.shape[0]
    u = _rms(x_ref[...], g_ref[...]).astype(MM_DTYPE)

    yf = _dot(u, wf_ref[...])
    f_ref[:, 0:FOX_W] = (yf[:, 0:FOX_W] * Q_SCALE).astype(f_ref.dtype)
    f_ref[:, FOX_W:2 * FOX_W] = yf[:, FOX_W:2 * FOX_W].astype(f_ref.dtype)
    for h in range(FOX_HEADS):
        v = yf[:, 2 * FOX_W + h * HEAD_DIM:2 * FOX_W + (h + 1) * HEAD_DIM]
        vtf_ref[h * HEAD_DIM:(h + 1) * HEAD_DIM, :] = v.T.astype(vtf_ref.dtype)

    yd = _dot(u, wd_ref[...])
    cd, sad, sbd = cd_ref[...], sad_ref[...], sbd_ref[...]
    for j in range(DSA_HEADS + 1):
        seg = _rope(yd[:, j * LANES:(j + 1) * LANES], cd, sad, sbd, HEAD_DIM // ROT_FRAC_DEN // 2)
        if j < DSA_HEADS:
            seg = seg * Q_SCALE
        d_ref[:, j * LANES:(j + 1) * LANES] = seg.astype(d_ref.dtype)
    vtd_ref[...] = yd[:, (DSA_HEADS + 1) * LANES:].T.astype(vtd_ref.dtype)

    yi = _dot(u, wi_ref[...])
    ci, sai, sbi = ci_ref[...], sai_ref[...], sbi_ref[...]
    for j in range(yi.shape[1] // LANES):
        seg = _rope(yi[:, j * LANES:(j + 1) * LANES], ci, sai, sbi, IDX_DIM // ROT_FRAC_DEN // 2)
        i_ref[:, j * LANES:(j + 1) * LANES] = seg.astype(i_ref.dtype)

    gate_ref[...] = _sigmoid(_dot(u, wg_ref[...]) + bg_ref[...]).astype(gate_ref.dtype)

    ym = _dot(u, wm_ref[...])
    col = lax.broadcasted_iota(jnp.int32, ym.shape, 1)
    row = lax.broadcasted_iota(jnp.int32, ym.shape, 0)
    is_f = (col >= MISC_F0) & (col < MISC_F0 + FOX_HEADS)
    z = ym + bm_ref[...]
    log_f = jnp.where(is_f, jnp.minimum(z, 0.0) - jnp.log1p(jnp.exp(-jnp.abs(z))), 0.0)

    @pl.when(pl.program_id(1) == 0)
    def _():
        carry_ref[...] = jnp.zeros_like(carry_ref)

    c = log_f
    shift = 1
    while shift < tm:
        c = c + jnp.where(row >= shift, pltpu.roll(c, shift, axis=0), 0.0)
        shift *= 2
    c = c + carry_ref[0:1, :]
    carry_ref[0:1, :] = c[tm - 1:tm, :]
    m_ref[...] = jnp.where(is_f, c * LOG2E, ym * idx_scale)


def _input_stage(x, gain, wf, wd, wi, wm, wg, bm, bg, tabs, *, tm):
    B, S, D = x.shape
    idx_scale = (IDX_DIM ** -0.5) * (IDX_HEADS ** -0.5)
    full = lambda a: pl.BlockSpec(a.shape, lambda b, t: (0,) * a.ndim)
    tab = pl.BlockSpec((tm, LANES), lambda b, t: (t, 0))
    rows = lambda w: pl.BlockSpec((None, tm, w), lambda b, t: (b, t, 0))
    cols = lambda h: pl.BlockSpec((None, h, tm), lambda b, t: (b, 0, t))
    qk_d_w = DSA_W + HEAD_DIM
    return pl.pallas_call(
        functools.partial(_input_kernel, idx_scale=idx_scale),
        grid=(B, S // tm),
        in_specs=[rows(D), full(gain), full(wf), full(wd), full(wi), full(wm), full(wg),
                  full(bm), full(bg)] + [tab] * 6,
        out_specs=[rows(2 * FOX_W), cols(FOX_W), rows(qk_d_w), cols(HEAD_DIM),
                   rows(wi.shape[1]), rows(wm.shape[1]), rows(wg.shape[1])],
        out_shape=[jax.ShapeDtypeStruct((B, S, 2 * FOX_W), MM_DTYPE),
                   jax.ShapeDtypeStruct((B, FOX_W, S), MM_DTYPE),
                   jax.ShapeDtypeStruct((B, S, qk_d_w), MM_DTYPE),
                   jax.ShapeDtypeStruct((B, HEAD_DIM, S), MM_DTYPE),
                   jax.ShapeDtypeStruct((B, S, wi.shape[1]), MM_DTYPE),
                   jax.ShapeDtypeStruct((B, S, wm.shape[1]), F32),
                   jax.ShapeDtypeStruct((B, S, wg.shape[1]), MM_DTYPE)],
        scratch_shapes=[pltpu.VMEM((8, LANES), F32)],
        compiler_params=pltpu.CompilerParams(
            dimension_semantics=("arbitrary", "arbitrary"), vmem_limit_bytes=VMEM_LIMIT),
        name="input_stage",
    )(x, gain, wf, wd, wi, wm, wg, bm, bg, *tabs)


def _flash_attend(i, qk_fn, mask_fn, pv_fn, o_ref, scratch, heads, tq, side_fn=None):
    s_buf, p_buf, m_s, l_s, a_s, acc_s = scratch
    m_s[...] = jnp.full(m_s.shape, NEG_BIG, F32)
    l_s[...] = jnp.zeros(l_s.shape, F32)
    a_s[...] = jnp.ones(a_s.shape, F32)
    acc_s[...] = jnp.zeros(acc_s.shape, F32)
    p_buf[1] = jnp.zeros(p_buf.shape[1:], p_buf.dtype)
    s_buf[0] = qk_fn(0)

    def lagged_values(c, slot):
        acc_s[...] = a_s[0:1, :] * acc_s[...] + pv_fn(c, p_buf[slot])

    def stage(c, cur, last):
        lagged_values(jnp.maximum(c - 1, 0), 1 - cur)
        if not last:
            s_buf[1 - cur] = qk_fn(c + 1)
        if side_fn is not None:
            side_fn(c, last)
        s = mask_fn(c, s_buf[cur], last)
        m_old = m_s[0:1, :]
        m_new = jnp.maximum(m_old, jnp.max(s, axis=0, keepdims=True))
        alpha = jnp.exp2(m_old - m_new)
        p = jnp.exp2(s - m_new)
        l_s[0:1, :] = alpha * l_s[0:1, :] + jnp.sum(p, axis=0, keepdims=True)
        p_buf[cur] = p.astype(p_buf.dtype)
        a_s[0:1, :] = alpha
        m_s[0:1, :] = m_new

    def pair(j, _):
        stage(2 * j, 0, False)
        stage(2 * j + 1, 1, False)
        return 0

    lax.fori_loop(0, i // 2, pair, 0)

    @pl.when(i % 2 == 0)
    def _():
        stage(i, 0, True)
        lagged_values(i, 0)

    @pl.when(i % 2 == 1)
    def _():
        stage(i - 1, 0, False)
        stage(i, 1, True)
        lagged_values(i, 1)

    for h in range(heads):
        cols = slice(h * tq, (h + 1) * tq)
        o_t = acc_s[:, cols] / l_s[0:1, cols]
        o_ref[:, h * HEAD_DIM:(h + 1) * HEAD_DIM] = o_t.T.astype(o_ref.dtype)


def _flash_scratch(heads, tq):
    w = heads * tq
    return [pltpu.VMEM((2, tq, w), F32),
            pltpu.VMEM((2, tq, w), MM_DTYPE),
            pltpu.VMEM((SUBLANES, w), F32),
            pltpu.VMEM((SUBLANES, w), F32),
            pltpu.VMEM((SUBLANES, w), F32),
            pltpu.VMEM((HEAD_DIM, w), F32)]


def _fox_fns(i, q_ref, k_ref, vt_ref, m_ref, tq):
    ch = tq
    H = FOX_HEADS
    causal_diag = (lax.broadcasted_iota(jnp.int32, (ch, H * tq), 0)
                   <= lax.broadcasted_iota(jnp.int32, (ch, H * tq), 1) % tq)

    def qk(c):
        start = pl.multiple_of(c * ch, ch)
        parts = []
        for h in range(H):
            hd = slice(h * HEAD_DIM, (h + 1) * HEAD_DIM)
            s = _nt_dot(k_ref[pl.ds(start, ch), hd], q_ref[:, hd])
            parts.append(s - m_ref[pl.ds(start, ch), MISC_F0 + h:MISC_F0 + h + 1])
        return jnp.concatenate(parts, axis=1)

    def mask(c, s, last):
        return jnp.where(causal_diag, s, NEG_BIG) if last else s

    def pv(c, p):
        start = pl.multiple_of(c * ch, ch)
        return jnp.concatenate(
            [_dot(vt_ref[h * HEAD_DIM:(h + 1) * HEAD_DIM, pl.ds(start, ch)],
                  p[:, h * tq:(h + 1) * tq]) for h in range(H)], axis=1)

    return qk, mask, pv


IDX_BITS = 13
SEARCH_MAX_ITERS = 256
SEARCH_FIXED_ITERS = 12
F32_TINY = 1.1754943508222875e-38


def _mixers_kernel(qf_ref, kf_ref, vtf_ref, m_ref, qi_ref, ki_ref, qd_ref, kd_ref, vtd_ref,
                   of_ref, od_ref, sc_ref, ext_ref, *flash_scratch, topk):
    i = pl.program_id(1)
    tq = qi_ref.shape[0]
    ch = tq
    n_ch = i + 1
    key_off = lax.broadcasted_iota(jnp.int32, (ch, tq), 0)
    q_off = lax.broadcasted_iota(jnp.int32, (ch, tq), 1)
    causal_diag = key_off <= q_off

    half_lane = lax.broadcasted_iota(jnp.int32, (tq, LANES), 1) // IDX_DIM
    q_heads = []
    for h in range(IDX_HEADS):
        pair = qi_ref[:, (h // 2) * LANES:(h // 2 + 1) * LANES]
        q_heads.append(jnp.where(half_lane == h % 2, pair, jnp.zeros_like(pair)))
    w_t = m_ref[pl.ds(pl.multiple_of(i * tq, tq), tq), :].T
    ext_ref[0:SUBLANES, :] = jnp.full((SUBLANES, tq), -jnp.inf, F32)
    ext_ref[SUBLANES:2 * SUBLANES, :] = jnp.full((SUBLANES, tq), jnp.inf, F32)

    def score_chunk(c, diag):
        start = pl.multiple_of(c * ch, ch)
        kc = ki_ref[pl.ds(start, ch), :]
        acc = jnp.zeros((ch, tq), F32)
        for h in range(IDX_HEADS):
            acc = acc + jnp.maximum(_nt_dot(kc, q_heads[h]), 0.0) * w_t[h:h + 1, :]
        lo_src = acc
        if diag:
            lo_src = jnp.where(causal_diag, acc, jnp.inf)
            acc = jnp.where(causal_diag, acc, -jnp.inf)
        sc_ref[pl.ds(start, ch), :] = acc
        ext_ref[0:SUBLANES, :] = jnp.maximum(ext_ref[0:SUBLANES, :],
                                             _sublane_fold(acc, jnp.maximum))
        ext_ref[SUBLANES:2 * SUBLANES, :] = jnp.minimum(ext_ref[SUBLANES:2 * SUBLANES, :],
                                                        _sublane_fold(lo_src, jnp.minimum))

    fox_qk, fox_mask, fox_pv = _fox_fns(i, qf_ref, kf_ref, vtf_ref, m_ref, tq)
    _flash_attend(i, fox_qk, fox_mask, fox_pv, of_ref, flash_scratch, FOX_HEADS, tq,
                  side_fn=score_chunk)

    row_max = jnp.max(ext_ref[0:SUBLANES, :], axis=0, keepdims=True)
    row_min = jnp.min(ext_ref[SUBLANES:2 * SUBLANES, :], axis=0, keepdims=True)

    def count(pred):
        def hits(c):
            start = pl.multiple_of(c * ch, ch)
            hit = pred(sc_ref[pl.ds(start, ch), :], start)
            return _sublane_fold(jnp.where(hit, 1.0, 0.0), jnp.add)
        zero = jnp.zeros((SUBLANES, tq), F32)
        part = lax.fori_loop(0, n_ch // 2, lambda j, cnt: cnt + hits(2 * j) + hits(2 * j + 1), zero)
        part = part + lax.cond(n_ch % 2 == 1, lambda: hits(n_ch - 1), lambda: zero)
        return jnp.sum(part, axis=0, keepdims=True)

    kf = jnp.float32(topk)
    n_causal = (i * tq + 1 + lax.broadcasted_iota(jnp.int32, (1, tq), 1)).astype(F32)
    few = n_causal < kf

    def active(lo, hi, c_lo):
        mid = 0.5 * lo + 0.5 * hi
        return (c_lo > kf) & (mid > lo) & (mid < hi)

    def any_active(state):
        lo, hi, c_lo, it = state[0], state[1], state[2], state[-1]
        busy = jnp.max(jnp.where(active(lo, hi, c_lo), 1.0, 0.0))
        return (busy > 0.5) & (it < SEARCH_MAX_ITERS)

    log_k = jnp.log(kf - 0.5)

    def narrow(state):
        lo, hi, c_lo, c_hi, w_lo, w_hi, last, it = state
        act = active(lo, hi, c_lo)
        mid = 0.5 * lo + 0.5 * hi
        f_lo = (jnp.log(c_lo) - log_k) * w_lo
        f_hi = (log_k - jnp.log(jnp.maximum(c_hi, 0.25))) * w_hi
        guess = lo + (hi - lo) * (f_lo / (f_lo + f_hi))
        cand = jnp.where((guess > lo) & (guess < hi), guess, mid)
        cand = jnp.where((lo < 0.0) & (hi > 0.0), 0.0, cand)
        cand = jnp.where((lo == 0.0) & (hi > F32_TINY), F32_TINY, cand)
        cb = jnp.broadcast_to(cand, (ch, tq))
        cnt = count(lambda s, _: s >= cb)
        up = act & (cnt >= kf)
        dn = act & (cnt < kf)
        w_lo = jnp.where(up, 1.0, jnp.where(dn, jnp.where(last < 0.0, 0.5 * w_lo, 1.0), w_lo))
        w_hi = jnp.where(dn, 1.0, jnp.where(up, jnp.where(last > 0.0, 0.5 * w_hi, 1.0), w_hi))
        last = jnp.where(up, 1.0, jnp.where(dn, -1.0, last))
        return (jnp.where(up, cand, lo), jnp.where(dn, cand, hi),
                jnp.where(up, cnt, c_lo), jnp.where(dn, cnt, c_hi), w_lo, w_hi, last, it + 1)

    above_max = row_max + (row_max - row_min) + 1.0
    ones = jnp.ones((1, tq), F32)
    start_state = (row_min, above_max, jnp.where(few, kf, n_causal), jnp.zeros((1, tq), F32),
                   ones, ones, jnp.zeros((1, tq), F32), jnp.int32(0))
    state = lax.fori_loop(0, SEARCH_FIXED_ITERS, lambda _, st: narrow(st), start_state)
    lo, _, c_lo, c_hi = lax.while_loop(any_active, narrow, state)[:4]

    t = jnp.where(few, -jnp.inf, lo)
    tb = jnp.broadcast_to(t, (ch, tq))

    need = kf - c_hi

    @pl.when(jnp.max(c_lo) > kf)
    def _():
        def idx_step(n, jt):
            cand = jt | (jnp.int32(1) << (IDX_BITS - 1 - n))
            cb = jnp.broadcast_to(cand, (ch, tq))
            below = count(lambda s, start: (s == tb) & (start + key_off < cb))
            return jnp.where(below <= need, cand, jt)
        jt = lax.fori_loop(0, IDX_BITS, idx_step, jnp.zeros((1, tq), jnp.int32))
        jt = jnp.where(c_lo > kf, jt, 2 ** IDX_BITS - 1)
        jb = jnp.broadcast_to(jt, (ch, tq))

        def drop_surplus(c, _):
            start = pl.multiple_of(c * ch, ch)
            s = sc_ref[pl.ds(start, ch), :]
            surplus = (s == tb) & (start + key_off >= jb)
            sc_ref[pl.ds(start, ch), :] = jnp.where(surplus, -jnp.inf, s)
            return 0
        lax.fori_loop(0, n_ch, drop_surplus, 0)

    q_all = jnp.concatenate([qd_ref[:, h * HEAD_DIM:(h + 1) * HEAD_DIM]
                             for h in range(DSA_HEADS)], axis=0)

    def qk(c):
        return _nt_dot(kd_ref[pl.ds(pl.multiple_of(c * ch, ch), ch), :], q_all)

    def mask(c, s, last):
        start = pl.multiple_of(c * ch, ch)
        sel = sc_ref[pl.ds(start, ch), :] >= tb
        if last:
            sel = sel & causal_diag
        return jnp.concatenate([jnp.where(sel, s[:, h * tq:(h + 1) * tq], NEG_BIG)
                                for h in range(DSA_HEADS)], axis=1)

    def pv(c, p):
        return _dot(vtd_ref[:, pl.ds(pl.multiple_of(c * ch, ch), ch)], p)

    _flash_attend(i, qk, mask, pv, od_ref, flash_scratch, DSA_HEADS, tq)


def _token_mixers(qk_f, vt_f, misc, qk_i, qk_d, vt_d, *, tq, topk):
    B, S, _ = qk_d.shape
    assert FOX_HEADS == DSA_HEADS and FOX_W == DSA_W
    blk = lambda w, col: pl.BlockSpec((None, tq, w), lambda b, i: (b, i, col))
    res = lambda w, col: pl.BlockSpec((None, S, w), lambda b, i: (b, 0, col))
    res_t = lambda h: pl.BlockSpec((None, h, S), lambda b, i: (b, 0, 0))
    return pl.pallas_call(
        functools.partial(_mixers_kernel, topk=topk),
        grid=(B, S // tq),
        in_specs=[blk(FOX_W, 0), res(FOX_W, 1), res_t(FOX_W), res(LANES, 0),
                  blk(IDX_W, 0), res(LANES, IDX_W // LANES),
                  blk(DSA_W, 0), res(LANES, DSA_W // LANES), res_t(HEAD_DIM)],
        out_specs=[blk(FOX_W, 0), blk(DSA_W, 0)],
        out_shape=[jax.ShapeDtypeStruct((B, S, FOX_W), MM_DTYPE),
                   jax.ShapeDtypeStruct((B, S, DSA_W), MM_DTYPE)],
        scratch_shapes=[pltpu.VMEM((S, tq), F32),
                        pltpu.VMEM((2 * SUBLANES, tq), F32)]
                       + _flash_scratch(DSA_HEADS, tq),
        compiler_params=pltpu.CompilerParams(
            dimension_semantics=("arbitrary", "arbitrary"), vmem_limit_bytes=VMEM_LIMIT),
        name="token_mixers",
    )(qk_f, qk_f, vt_f, misc, qk_i, qk_i, qk_d, qk_d, vt_d)


def _merge_kernel(of_ref, od_ref, gate_ref, x_ref, wbf_ref, wbd_ref, wo_ref, g_ref, h_ref):
    a = _dot(of_ref[...], wbf_ref[...])
    b = _dot(od_ref[...], wbd_ref[...])
    mixed = (gate_ref[:, 0:D_MODEL].astype(F32) * a
             + gate_ref[:, D_MODEL:2 * D_MODEL].astype(F32) * b)
    y = _dot(mixed.astype(MM_DTYPE), wo_ref[...])
    h_ref[...] = x_ref[...] + _rms(y, g_ref[...])


def _merge_stage(o_f, o_d, gates, x, wbf, wbd, wo, g_post, *, tm):
    N, D = x.shape
    row = lambda w: pl.BlockSpec((tm, w), lambda t: (t, 0))
    full = lambda a: pl.BlockSpec(a.shape, lambda t: (0,) * a.ndim)
    return pl.pallas_call(
        _merge_kernel,
        grid=(N // tm,),
        in_specs=[row(FOX_W), row(DSA_W), row(2 * D), row(D),
                  full(wbf), full(wbd), full(wo), full(g_post)],
        out_specs=row(D),
        out_shape=jax.ShapeDtypeStruct((N, D), F32),
        compiler_params=pltpu.CompilerParams(
            dimension_semantics=("arbitrary",), vmem_limit_bytes=VMEM_LIMIT),
        name="merge_stage",
    )(o_f, o_d, gates, x, wbf, wbd, wo, g_post)


def _ffn_kernel(h_ref, gpre_ref, wg_ref, wu_ref, wd_ref, gpost_ref, o_ref, v_ref, acc_ref):
    j = pl.program_id(1)

    @pl.when(j == 0)
    def _():
        v_ref[...] = _rms(h_ref[...], gpre_ref[...]).astype(v_ref.dtype)
        acc_ref[...] = jnp.zeros_like(acc_ref)

    v = v_ref[...]
    g = _dot(v, wg_ref[...])
    a = (g * _sigmoid(g) * _dot(v, wu_ref[...])).astype(MM_DTYPE)
    acc_ref[...] += _dot(a, wd_ref[...])

    @pl.when(j == pl.num_programs(1) - 1)
    def _():
        o_ref[...] = h_ref[...] + _rms(acc_ref[...], gpost_ref[...])


def _ffn_stage(h, g_pre, wg, wu, wd, g_post, *, tm, tf):
    N, D = h.shape
    F = wg.shape[1]
    full = lambda a: pl.BlockSpec(a.shape, lambda t, j: (0,) * a.ndim)
    return pl.pallas_call(
        _ffn_kernel,
        grid=(N // tm, F // tf),
        in_specs=[pl.BlockSpec((tm, D), lambda t, j: (t, 0)), full(g_pre),
                  pl.BlockSpec((D, tf), lambda t, j: (0, j)),
                  pl.BlockSpec((D, tf), lambda t, j: (0, j)),
                  pl.BlockSpec((tf, D), lambda t, j: (j, 0)), full(g_post)],
        out_specs=pl.BlockSpec((tm, D), lambda t, j: (t, 0)),
        out_shape=jax.ShapeDtypeStruct((N, D), F32),
        scratch_shapes=[pltpu.VMEM((tm, D), MM_DTYPE), pltpu.VMEM((tm, D), F32)],
        compiler_params=pltpu.CompilerParams(
            dimension_semantics=("arbitrary", "arbitrary"), vmem_limit_bytes=VMEM_LIMIT),
        name="ffn_stage",
    )(h, g_pre, wg, wu, wd, g_post)


def _rope_tables(S, rot, period):
    half = rot // 2
    inv_freq = jnp.float32(ROPE_THETA) ** (-jnp.arange(half, dtype=F32) * 2.0 / rot)
    ang = jnp.arange(S).astype(F32)[:, None] * inv_freq[None, :]
    cos, sin = jnp.cos(ang), jnp.sin(ang)
    one = jnp.ones((S, period - rot), F32)
    zero = jnp.zeros((S, period - rot), F32)
    zh = jnp.zeros((S, half), F32)
    reps = LANES // period
    cos_t = jnp.tile(jnp.concatenate([cos, cos, one], axis=1), (1, reps))
    sin_a = jnp.tile(jnp.concatenate([-sin, zh, zero], axis=1), (1, reps))
    sin_b = jnp.tile(jnp.concatenate([zh, sin, zero], axis=1), (1, reps))
    return cos_t, sin_a, sin_b


def _split_w_in(w):
    parts, off = [], 0
    for width in IN_WIDTHS:
        parts.append(w[:, off:off + width])
        off += width
    return parts


def _layer(h, p, tabs, *, tm_in, tq_mix, tm_merge, tm_ffn, tf):
    B, S, D = h.shape
    (w_qf, w_kf, w_vf, w_fl, w_qd, w_kd, w_vd, w_qi, w_ki, w_wi, w_gf, w_gd) = _split_w_in(p["w_in"])
    cast = lambda a: a.astype(MM_DTYPE)
    pad_m = jnp.zeros((D, LANES - IDX_HEADS - FOX_HEADS), F32)
    wf = cast(jnp.concatenate([w_qf, w_kf, w_vf], axis=1))
    wd = cast(jnp.concatenate([w_qd, w_kd, w_vd], axis=1))
    wi = cast(jnp.concatenate([w_qi, w_ki, w_ki], axis=1))
    wm = cast(jnp.concatenate([w_wi, w_fl, pad_m], axis=1))
    wg = cast(jnp.concatenate([w_gf, w_gd], axis=1))
    bm = jnp.zeros((1, LANES), F32).at[0, MISC_F0:MISC_F0 + FOX_HEADS].set(p["b_forget"].astype(F32))
    bg = p["b_gate"].astype(F32).reshape(1, 2 * D)
    row = lambda a: a.astype(F32).reshape(1, D)

    qk_f, vt_f, qk_d, vt_d, qk_i, misc, gates = _input_stage(
        h, row(p["norm_mix_pre"]), wf, wd, wi, wm, wg, bm, bg, tabs, tm=tm_in)

    o_f, o_d = _token_mixers(qk_f, vt_f, misc, qk_i, qk_d, vt_d, tq=tq_mix,
                             topk=min(TOPK_MAX, S // 4))

    N = B * S
    h1 = _merge_stage(o_f.reshape(N, FOX_W), o_d.reshape(N, DSA_W), gates.reshape(N, 2 * D),
                      h.reshape(N, D), cast(p["w_branch_fox"]), cast(p["w_branch_dsa"]),
                      cast(p["w_out"]), row(p["norm_mix_post"]), tm=tm_merge)
    h2 = _ffn_stage(h1, row(p["norm_ffn_pre"]), cast(p["w_ffn_gate"]), cast(p["w_ffn_up"]),
                    cast(p["w_ffn_down"]), row(p["norm_ffn_post"]), tm=tm_ffn, tf=tf)
    return h2.reshape(B, S, D)


def kernel(x, norm_mix_pre, w_in, b_forget, b_gate, w_branch_fox, w_branch_dsa, w_out,
           norm_mix_post, norm_ffn_pre, w_ffn_gate, w_ffn_up, w_ffn_down, norm_ffn_post):
    B, S, D = x.shape
    params = dict(norm_mix_pre=norm_mix_pre, w_in=w_in, b_forget=b_forget, b_gate=b_gate,
                  w_branch_fox=w_branch_fox, w_branch_dsa=w_branch_dsa, w_out=w_out,
                  norm_mix_post=norm_mix_post, norm_ffn_pre=norm_ffn_pre, w_ffn_gate=w_ffn_gate,
                  w_ffn_up=w_ffn_up, w_ffn_down=w_ffn_down, norm_ffn_post=norm_ffn_post)
    tabs = (_rope_tables(S, HEAD_DIM // ROT_FRAC_DEN, HEAD_DIM)
            + _rope_tables(S, IDX_DIM // ROT_FRAC_DEN, IDX_DIM))
    tiles = dict(tm_in=min(512, S), tq_mix=256,
                 tm_merge=min(512, B * S), tm_ffn=min(1024, B * S), tf=D_FF // 2)
    h = x
    for l in range(w_in.shape[0]):
        h = _layer(h, {k: v[l] for k, v in params.items()}, tabs, **tiles)
    return h
```

```python
import functools

import jax
import jax.numpy as jnp
from jax import lax
from jax.experimental import pallas as pl
from jax.experimental.pallas import tpu as pltpu

D_MODEL = 1024
HEAD_DIM = 128
FOX_HEADS = 4
DSA_HEADS = 4
IDX_HEADS = 8
IDX_DIM = 64
ROT_FRAC_DEN = 4
ROPE_THETA = 500000.0
TOPK_MAX = 256
D_FF = 2816
RMS_EPS = 1e-6
FOX_W = FOX_HEADS * HEAD_DIM
DSA_W = DSA_HEADS * HEAD_DIM
IDX_W = IDX_HEADS * IDX_DIM
IN_WIDTHS = (FOX_W, FOX_W, FOX_W, FOX_HEADS, DSA_W, HEAD_DIM, HEAD_DIM,
             IDX_W, IDX_DIM, IDX_HEADS, D_MODEL, D_MODEL)

LANES = 128
SUBLANES = 8
VMEM_LIMIT = 56 * 1024 * 1024
MM_DTYPE = jnp.bfloat16
NEG_BIG = -1e30
LOG2E = 1.4426950408889634
Q_SCALE = HEAD_DIM ** -0.5 * LOG2E

F32 = jnp.float32


def _nt_dot(a, b):
    return lax.dot_general(a, b, (((1,), (1,)), ((), ())), preferred_element_type=F32)


def _dot(a, b):
    return jnp.dot(a, b, preferred_element_type=F32)


def _rms(x, g):
    return x * lax.rsqrt(jnp.mean(x * x, axis=-1, keepdims=True) + RMS_EPS) * g


def _sigmoid(z):
    return 1.0 / (1.0 + jnp.exp(-z))


def _sublane_fold(x, op):
    acc = x[0:SUBLANES, :]
    for j in range(1, x.shape[0] // SUBLANES):
        acc = op(acc, x[j * SUBLANES:(j + 1) * SUBLANES, :])
    return acc


MISC_F0 = IDX_HEADS


def _rope(seg, cos, sin_a, sin_b, half):
    return (seg * cos
            + pltpu.roll(seg, LANES - half, axis=1) * sin_a
            + pltpu.roll(seg, half, axis=1) * sin_b)


def _input_kernel(x_ref, g_ref, wf_ref, wd_ref, wi_ref, wm_ref, wg_ref, bm_ref, bg_ref,
                  cd_ref, sad_ref, sbd_ref, ci_ref, sai_ref, sbi_ref,
                  f_ref, vtf_ref, d_ref, vtd_ref, i_ref, m_ref, gate_ref, carry_ref, *, idx_scale):
    tm = x_ref.shape[0]
    u = _rms(x_ref[...], g_ref[...]).astype(MM_DTYPE)

    yf = _dot(u, wf_ref[...])
    f_ref[:, 0:FOX_W] = (yf[:, 0:FOX_W] * Q_SCALE).astype(f_ref.dtype)
    f_ref[:, FOX_W:2 * FOX_W] = yf[:, FOX_W:2 * FOX_W].astype(f_ref.dtype)
    for h in range(FOX_HEADS):
        v = yf[:, 2 * FOX_W + h * HEAD_DIM:2 * FOX_W + (h + 1) * HEAD_DIM]
        vtf_ref[h * HEAD_DIM:(h + 1) * HEAD_DIM, :] = v.T.astype(vtf_ref.dtype)

    yd = _dot(u, wd_ref[...])
    cd, sad, sbd = cd_ref[...], sad_ref[...], sbd_ref[...]
    for j in range(DSA_HEADS + 1):
        seg = _rope(yd[:, j * LANES:(j + 1) * LANES], cd, sad, sbd, HEAD_DIM // ROT_FRAC_DEN // 2)
        if j < DSA_HEADS:
            seg = seg * Q_SCALE
        d_ref[:, j * LANES:(j + 1) * LANES] = seg.astype(d_ref.dtype)
    vtd_ref[...] = yd[:, (DSA_HEADS + 1) * LANES:].T.astype(vtd_ref.dtype)

    yi = _dot(u, wi_ref[...])
    ci, sai, sbi = ci_ref[...], sai_ref[...], sbi_ref[...]
    for j in range(yi.shape[1] // LANES):
        seg = _rope(yi[:, j * LANES:(j + 1) * LANES], ci, sai, sbi, IDX_DIM // ROT_FRAC_DEN // 2)
        i_ref[:, j * LANES:(j + 1) * LANES] = seg.astype(i_ref.dtype)

    gate_ref[...] = _sigmoid(_dot(u, wg_ref[...]) + bg_ref[...]).astype(gate_ref.dtype)

    ym = _dot(u, wm_ref[...])
    col = lax.broadcasted_iota(jnp.int32, ym.shape, 1)
    row = lax.broadcasted_iota(jnp.int32, ym.shape, 0)
    is_f = (col >= MISC_F0) & (col < MISC_F0 + FOX_HEADS)
    z = ym + bm_ref[...]
    log_f = jnp.where(is_f, jnp.minimum(z, 0.0) - jnp.log1p(jnp.exp(-jnp.abs(z))), 0.0)

    @pl.when(pl.program_id(1) == 0)
    def _():
        carry_ref[...] = jnp.zeros_like(carry_ref)

    c = log_f
    shift = 1
    while shift < tm:
        c = c + jnp.where(row >= shift, pltpu.roll(c, shift, axis=0), 0.0)
        shift *= 2
    c = c + carry_ref[0:1, :]
    carry_ref[0:1, :] = c[tm - 1:tm, :]
    m_ref[...] = jnp.where(is_f, c * LOG2E, ym * idx_scale)


def _input_stage(x, gain, wf, wd, wi, wm, wg, bm, bg, tabs, *, tm):
    B, S, D = x.shape
    idx_scale = (IDX_DIM ** -0.5) * (IDX_HEADS ** -0.5)
    full = lambda a: pl.BlockSpec(a.shape, lambda b, t: (0,) * a.ndim)
    tab = pl.BlockSpec((tm, LANES), lambda b, t: (t, 0))
    rows = lambda w: pl.BlockSpec((None, tm, w), lambda b, t: (b, t, 0))
    cols = lambda h: pl.BlockSpec((None, h, tm), lambda b, t: (b, 0, t))
    qk_d_w = DSA_W + HEAD_DIM
    return pl.pallas_call(
        functools.partial(_input_kernel, idx_scale=idx_scale),
        grid=(B, S // tm),
        in_specs=[rows(D), full(gain), full(wf), full(wd), full(wi), full(wm), full(wg),
                  full(bm), full(bg)] + [tab] * 6,
        out_specs=[rows(2 * FOX_W), cols(FOX_W), rows(qk_d_w), cols(HEAD_DIM),
                   rows(wi.shape[1]), rows(wm.shape[1]), rows(wg.shape[1])],
        out_shape=[jax.ShapeDtypeStruct((B, S, 2 * FOX_W), MM_DTYPE),
                   jax.ShapeDtypeStruct((B, FOX_W, S), MM_DTYPE),
                   jax.ShapeDtypeStruct((B, S, qk_d_w), MM_DTYPE),
                   jax.ShapeDtypeStruct((B, HEAD_DIM, S), MM_DTYPE),
                   jax.ShapeDtypeStruct((B, S, wi.shape[1]), MM_DTYPE),
                   jax.ShapeDtypeStruct((B, S, wm.shape[1]), F32),
                   jax.ShapeDtypeStruct((B, S, wg.shape[1]), MM_DTYPE)],
        scratch_shapes=[pltpu.VMEM((8, LANES), F32)],
        compiler_params=pltpu.CompilerParams(
            dimension_semantics=("arbitrary", "arbitrary"), vmem_limit_bytes=VMEM_LIMIT),
        name="input_stage",
    )(x, gain, wf, wd, wi, wm, wg, bm, bg, *tabs)


def _flash_attend(i, qk_fn, mask_fn, pv_fn, o_ref, scratch, heads, tq, side_fn=None):
    s_buf, p_buf, m_s, l_s, a_s, acc_s = scratch
    head_cols = [slice(h * tq, (h + 1) * tq) for h in range(heads)]
    m_s[...] = jnp.full(m_s.shape, NEG_BIG, F32)
    l_s[...] = jnp.zeros(l_s.shape, F32)
    a_s[...] = jnp.ones(a_s.shape, F32)
    acc_s[...] = jnp.zeros(acc_s.shape, F32)
    p_buf[1] = jnp.zeros(p_buf.shape[1:], p_buf.dtype)
    for h, hc in enumerate(head_cols):
        s_buf[0, :, hc] = qk_fn(0, h)

    def lagged_values(c, slot):
        for h, hc in enumerate(head_cols):
            acc_s[:, hc] = a_s[0:1, hc] * acc_s[:, hc] + pv_fn(c, h, p_buf[slot, :, hc])

    def stage(c, cur, last):
        lagged_values(jnp.maximum(c - 1, 0), 1 - cur)
        if not last:
            for h, hc in enumerate(head_cols):
                s_buf[1 - cur, :, hc] = qk_fn(c + 1, h)
        if side_fn is not None:
            side_fn(c, last)
        for j in range(heads * tq // LANES):
            cols = slice(j * LANES, (j + 1) * LANES)
            s = mask_fn(c, s_buf[cur, :, cols], last, j % (tq // LANES))
            m_old = m_s[0:1, cols]
            m_new = jnp.maximum(m_old, jnp.max(s, axis=0, keepdims=True))
            alpha = jnp.exp2(m_old - m_new)
            p = jnp.exp2(s - m_new)
            l_s[0:1, cols] = alpha * l_s[0:1, cols] + jnp.sum(p, axis=0, keepdims=True)
            p_buf[cur, :, cols] = p.astype(p_buf.dtype)
            a_s[0:1, cols] = alpha
            m_s[0:1, cols] = m_new

    def pair(j, _):
        stage(2 * j, 0, False)
        stage(2 * j + 1, 1, False)
        return 0

    lax.fori_loop(0, i // 2, pair, 0)

    @pl.when(i % 2 == 0)
    def _():
        stage(i, 0, True)
        lagged_values(i, 0)

    @pl.when(i % 2 == 1)
    def _():
        stage(i - 1, 0, False)
        stage(i, 1, True)
        lagged_values(i, 1)

    for h in range(heads):
        cols = slice(h * tq, (h + 1) * tq)
        o_t = acc_s[:, cols] / l_s[0:1, cols]
        o_ref[:, h * HEAD_DIM:(h + 1) * HEAD_DIM] = o_t.T.astype(o_ref.dtype)


def _flash_scratch(heads, tq):
    w = heads * tq
    return [pltpu.VMEM((2, tq, w), F32),
            pltpu.VMEM((2, tq, w), MM_DTYPE),
            pltpu.VMEM((SUBLANES, w), F32),
            pltpu.VMEM((SUBLANES, w), F32),
            pltpu.VMEM((SUBLANES, w), F32),
            pltpu.VMEM((HEAD_DIM, w), F32)]


def _causal_tiles(ch, tq):
    key_off = lax.broadcasted_iota(jnp.int32, (ch, LANES), 0)
    q_off = lax.broadcasted_iota(jnp.int32, (ch, LANES), 1)
    return [key_off <= q_off + t * LANES for t in range(tq // LANES)]


def _fox_fns(q_ref, k_ref, vt_ref, m_ref, tq):
    ch = tq
    causal = _causal_tiles(ch, tq)

    def qk(c, h):
        start = pl.multiple_of(c * ch, ch)
        hd = slice(h * HEAD_DIM, (h + 1) * HEAD_DIM)
        s = _nt_dot(k_ref[pl.ds(start, ch), hd], q_ref[:, hd])
        return s - m_ref[pl.ds(start, ch), MISC_F0 + h:MISC_F0 + h + 1]

    def mask(c, s, last, t):
        return jnp.where(causal[t], s, NEG_BIG) if last else s

    def pv(c, h, p):
        start = pl.multiple_of(c * ch, ch)
        return _dot(vt_ref[h * HEAD_DIM:(h + 1) * HEAD_DIM, pl.ds(start, ch)], p)

    return qk, mask, pv


IDX_BITS = 13
SEARCH_MAX_ITERS = 256
SEARCH_FIXED_ITERS = 14
F32_TINY = 1.1754943508222875e-38


def _mixers_kernel(qf_ref, kf_ref, vtf_ref, m_ref, qi_ref, ki_ref, qd_ref, kd_ref, vtd_ref,
                   of_ref, od_ref, sc_ref, ext_ref, *flash_scratch, topk):
    i = pl.program_id(1)
    tq = qi_ref.shape[0]
    ch = tq
    n_ch = i + 1
    key_off = lax.broadcasted_iota(jnp.int32, (ch, tq), 0)
    q_off = lax.broadcasted_iota(jnp.int32, (ch, tq), 1)
    causal_diag = key_off <= q_off
    causal_tiles = _causal_tiles(ch, tq)

    half_lane = lax.broadcasted_iota(jnp.int32, (tq, LANES), 1) // IDX_DIM
    q_heads = []
    for h in range(IDX_HEADS):
        pair = qi_ref[:, (h // 2) * LANES:(h // 2 + 1) * LANES]
        q_heads.append(jnp.where(half_lane == h % 2, pair, jnp.zeros_like(pair)))
    w_t = m_ref[pl.ds(pl.multiple_of(i * tq, tq), tq), :].T
    ext_ref[0:SUBLANES, :] = jnp.full((SUBLANES, tq), -jnp.inf, F32)
    ext_ref[SUBLANES:2 * SUBLANES, :] = jnp.full((SUBLANES, tq), jnp.inf, F32)

    def score_chunk(c, diag):
        start = pl.multiple_of(c * ch, ch)
        kc = ki_ref[pl.ds(start, ch), :]
        acc = jnp.zeros((ch, tq), F32)
        for h in range(IDX_HEADS):
            acc = acc + jnp.maximum(_nt_dot(kc, q_heads[h]), 0.0) * w_t[h:h + 1, :]
        lo_src = acc
        if diag:
            lo_src = jnp.where(causal_diag, acc, jnp.inf)
            acc = jnp.where(causal_diag, acc, -jnp.inf)
        sc_ref[pl.ds(start, ch), :] = acc
        ext_ref[0:SUBLANES, :] = jnp.maximum(ext_ref[0:SUBLANES, :],
                                             _sublane_fold(acc, jnp.maximum))
        ext_ref[SUBLANES:2 * SUBLANES, :] = jnp.minimum(ext_ref[SUBLANES:2 * SUBLANES, :],
                                                        _sublane_fold(lo_src, jnp.minimum))

    fox_qk, fox_mask, fox_pv = _fox_fns(qf_ref, kf_ref, vtf_ref, m_ref, tq)
    _flash_attend(i, fox_qk, fox_mask, fox_pv, of_ref, flash_scratch, FOX_HEADS, tq,
                  side_fn=score_chunk)

    row_max = jnp.max(ext_ref[0:SUBLANES, :], axis=0, keepdims=True)
    row_min = jnp.min(ext_ref[SUBLANES:2 * SUBLANES, :], axis=0, keepdims=True)

    def count(pred):
        def hits(c):
            start = pl.multiple_of(c * ch, ch)
            hit = pred(sc_ref[pl.ds(start, ch), :], start)
            return _sublane_fold(jnp.where(hit, 1.0, 0.0), jnp.add)
        zero = jnp.zeros((SUBLANES, tq), F32)
        part = lax.fori_loop(0, n_ch // 2, lambda j, cnt: cnt + hits(2 * j) + hits(2 * j + 1), zero)
        part = part + lax.cond(n_ch % 2 == 1, lambda: hits(n_ch - 1), lambda: zero)
        return jnp.sum(part, axis=0, keepdims=True)

    kf = jnp.float32(topk)
    n_causal = (i * tq + 1 + lax.broadcasted_iota(jnp.int32, (1, tq), 1)).astype(F32)
    few = n_causal < kf

    def active(lo, hi, c_lo):
        mid = 0.5 * lo + 0.5 * hi
        return (c_lo > kf) & (mid > lo) & (mid < hi)

    def any_active(state):
        lo, hi, c_lo, it = state[0], state[1], state[2], state[-1]
        busy = jnp.max(jnp.where(active(lo, hi, c_lo), 1.0, 0.0))
        return (busy > 0.5) & (it < SEARCH_MAX_ITERS)

    log_k = jnp.log(kf - 0.5)

    def narrow(state):
        lo, hi, c_lo, c_hi, w_lo, w_hi, last, it = state
        act = active(lo, hi, c_lo)
        mid = 0.5 * lo + 0.5 * hi
        f_lo = (jnp.log(c_lo) - log_k) * w_lo
        f_hi = (log_k - jnp.log(jnp.maximum(c_hi, 0.25))) * w_hi
        guess = lo + (hi - lo) * (f_lo / (f_lo + f_hi))
        cand = jnp.where((guess > lo) & (guess < hi), guess, mid)
        cand = jnp.where((lo < 0.0) & (hi > 0.0), 0.0, cand)
        cand = jnp.where((lo == 0.0) & (hi > F32_TINY), F32_TINY, cand)
        cb = jnp.broadcast_to(cand, (ch, tq))
        cnt = count(lambda s, _: s >= cb)
        up = act & (cnt >= kf)
        dn = act & (cnt < kf)
        w_lo = jnp.where(up, 1.0, jnp.where(dn, jnp.where(last < 0.0, 0.5 * w_lo, 1.0), w_lo))
        w_hi = jnp.where(dn, 1.0, jnp.where(up, jnp.where(last > 0.0, 0.5 * w_hi, 1.0), w_hi))
        last = jnp.where(up, 1.0, jnp.where(dn, -1.0, last))
        return (jnp.where(up, cand, lo), jnp.where(dn, cand, hi),
                jnp.where(up, cnt, c_lo), jnp.where(dn, cnt, c_hi), w_lo, w_hi, last, it + 1)

    above_max = row_max + (row_max - row_min) + 1.0
    ones = jnp.ones((1, tq), F32)
    start_state = (row_min, above_max, jnp.where(few, kf, n_causal), jnp.zeros((1, tq), F32),
                   ones, ones, jnp.zeros((1, tq), F32), jnp.int32(0))
    state = lax.fori_loop(0, SEARCH_FIXED_ITERS, lambda _, st: narrow(st), start_state)
    lo, _, c_lo, c_hi = lax.while_loop(any_active, narrow, state)[:4]

    t = jnp.where(few, -jnp.inf, lo)
    tb = jnp.broadcast_to(t, (ch, tq))

    need = kf - c_hi

    @pl.when(jnp.max(c_lo) > kf)
    def _():
        def idx_step(n, jt):
            cand = jt | (jnp.int32(1) << (IDX_BITS - 1 - n))
            cb = jnp.broadcast_to(cand, (ch, tq))
            below = count(lambda s, start: (s == tb) & (start + key_off < cb))
            return jnp.where(below <= need, cand, jt)
        jt = lax.fori_loop(0, IDX_BITS, idx_step, jnp.zeros((1, tq), jnp.int32))
        jt = jnp.where(c_lo > kf, jt, 2 ** IDX_BITS - 1)
        jb = jnp.broadcast_to(jt, (ch, tq))

        def drop_surplus(c, _):
            start = pl.multiple_of(c * ch, ch)
            s = sc_ref[pl.ds(start, ch), :]
            surplus = (s == tb) & (start + key_off >= jb)
            sc_ref[pl.ds(start, ch), :] = jnp.where(surplus, -jnp.inf, s)
            return 0
        lax.fori_loop(0, n_ch, drop_surplus, 0)

    def qk(c, h):
        return _nt_dot(kd_ref[pl.ds(pl.multiple_of(c * ch, ch), ch), :],
                       qd_ref[:, h * HEAD_DIM:(h + 1) * HEAD_DIM])

    def mask(c, s, last, tile):
        lanes = slice(tile * LANES, (tile + 1) * LANES)
        sel = sc_ref[pl.ds(pl.multiple_of(c * ch, ch), ch), lanes] >= t[:, lanes]
        if last:
            sel = sel & causal_tiles[tile]
        return jnp.where(sel, s, NEG_BIG)

    def pv(c, h, p):
        return _dot(vtd_ref[:, pl.ds(pl.multiple_of(c * ch, ch), ch)], p)

    _flash_attend(i, qk, mask, pv, od_ref, flash_scratch, DSA_HEADS, tq)


def _token_mixers(qk_f, vt_f, misc, qk_i, qk_d, vt_d, *, tq, topk):
    B, S, _ = qk_d.shape
    assert FOX_HEADS == DSA_HEADS and FOX_W == DSA_W
    blk = lambda w, col: pl.BlockSpec((None, tq, w), lambda b, i: (b, i, col))
    res = lambda w, col: pl.BlockSpec((None, S, w), lambda b, i: (b, 0, col))
    res_t = lambda h: pl.BlockSpec((None, h, S), lambda b, i: (b, 0, 0))
    return pl.pallas_call(
        functools.partial(_mixers_kernel, topk=topk),
        grid=(B, S // tq),
        in_specs=[blk(FOX_W, 0), res(FOX_W, 1), res_t(FOX_W), res(LANES, 0),
                  blk(IDX_W, 0), res(LANES, IDX_W // LANES),
                  blk(DSA_W, 0), res(LANES, DSA_W // LANES), res_t(HEAD_DIM)],
        out_specs=[blk(FOX_W, 0), blk(DSA_W, 0)],
        out_shape=[jax.ShapeDtypeStruct((B, S, FOX_W), MM_DTYPE),
                   jax.ShapeDtypeStruct((B, S, DSA_W), MM_DTYPE)],
        scratch_shapes=[pltpu.VMEM((S, tq), F32),
                        pltpu.VMEM((2 * SUBLANES, tq), F32)]
                       + _flash_scratch(DSA_HEADS, tq),
        compiler_params=pltpu.CompilerParams(
            dimension_semantics=("arbitrary", "arbitrary"), vmem_limit_bytes=VMEM_LIMIT),
        name="token_mixers",
    )(qk_f, qk_f, vt_f, misc, qk_i, qk_i, qk_d, qk_d, vt_d)


def _merge_kernel(of_ref, od_ref, gate_ref, x_ref, wbf_ref, wbd_ref, wo_ref, g_ref, h_ref):
    a = _dot(of_ref[...], wbf_ref[...])
    b = _dot(od_ref[...], wbd_ref[...])
    mixed = (gate_ref[:, 0:D_MODEL].astype(F32) * a
             + gate_ref[:, D_MODEL:2 * D_MODEL].astype(F32) * b)
    y = _dot(mixed.astype(MM_DTYPE), wo_ref[...])
    h_ref[...] = x_ref[...] + _rms(y, g_ref[...])


def _merge_stage(o_f, o_d, gates, x, wbf, wbd, wo, g_post, *, tm):
    N, D = x.shape
    row = lambda w: pl.BlockSpec((tm, w), lambda t: (t, 0))
    full = lambda a: pl.BlockSpec(a.shape, lambda t: (0,) * a.ndim)
    return pl.pallas_call(
        _merge_kernel,
        grid=(N // tm,),
        in_specs=[row(FOX_W), row(DSA_W), row(2 * D), row(D),
                  full(wbf), full(wbd), full(wo), full(g_post)],
        out_specs=row(D),
        out_shape=jax.ShapeDtypeStruct((N, D), F32),
        compiler_params=pltpu.CompilerParams(
            dimension_semantics=("arbitrary",), vmem_limit_bytes=VMEM_LIMIT),
        name="merge_stage",
    )(o_f, o_d, gates, x, wbf, wbd, wo, g_post)


def _ffn_kernel(h_ref, gpre_ref, wg_ref, wu_ref, wd_ref, gpost_ref, o_ref, v_ref, acc_ref):
    j = pl.program_id(1)

    @pl.when(j == 0)
    def _():
        v_ref[...] = _rms(h_ref[...], gpre_ref[...]).astype(v_ref.dtype)
        acc_ref[...] = jnp.zeros_like(acc_ref)

    v = v_ref[...]
    g = _dot(v, wg_ref[...])
    a = (g * _sigmoid(g) * _dot(v, wu_ref[...])).astype(MM_DTYPE)
    acc_ref[...] += _dot(a, wd_ref[...])

    @pl.when(j == pl.num_programs(1) - 1)
    def _():
        o_ref[...] = h_ref[...] + _rms(acc_ref[...], gpost_ref[...])


def _ffn_stage(h, g_pre, wg, wu, wd, g_post, *, tm, tf):
    N, D = h.shape
    F = wg.shape[1]
    full = lambda a: pl.BlockSpec(a.shape, lambda t, j: (0,) * a.ndim)
    return pl.pallas_call(
        _ffn_kernel,
        grid=(N // tm, F // tf),
        in_specs=[pl.BlockSpec((tm, D), lambda t, j: (t, 0)), full(g_pre),
                  pl.BlockSpec((D, tf), lambda t, j: (0, j)),
                  pl.BlockSpec((D, tf), lambda t, j: (0, j)),
                  pl.BlockSpec((tf, D), lambda t, j: (j, 0)), full(g_post)],
        out_specs=pl.BlockSpec((tm, D), lambda t, j: (t, 0)),
        out_shape=jax.ShapeDtypeStruct((N, D), F32),
        scratch_shapes=[pltpu.VMEM((tm, D), MM_DTYPE), pltpu.VMEM((tm, D), F32)],
        compiler_params=pltpu.CompilerParams(
            dimension_semantics=("arbitrary", "arbitrary"), vmem_limit_bytes=VMEM_LIMIT),
        name="ffn_stage",
    )(h, g_pre, wg, wu, wd, g_post)


def _rope_tables(S, rot, period):
    half = rot // 2
    inv_freq = jnp.float32(ROPE_THETA) ** (-jnp.arange(half, dtype=F32) * 2.0 / rot)
    ang = jnp.arange(S).astype(F32)[:, None] * inv_freq[None, :]
    cos, sin = jnp.cos(ang), jnp.sin(ang)
    one = jnp.ones((S, period - rot), F32)
    zero = jnp.zeros((S, period - rot), F32)
    zh = jnp.zeros((S, half), F32)
    reps = LANES // period
    cos_t = jnp.tile(jnp.concatenate([cos, cos, one], axis=1), (1, reps))
    sin_a = jnp.tile(jnp.concatenate([-sin, zh, zero], axis=1), (1, reps))
    sin_b = jnp.tile(jnp.concatenate([zh, sin, zero], axis=1), (1, reps))
    return cos_t, sin_a, sin_b


def _split_w_in(w):
    parts, off = [], 0
    for width in IN_WIDTHS:
        parts.append(w[:, off:off + width])
        off += width
    return parts


def _layer(h, p, tabs, *, tm_in, tq_mix, tm_merge, tm_ffn, tf):
    B, S, D = h.shape
    (w_qf, w_kf, w_vf, w_fl, w_qd, w_kd, w_vd, w_qi, w_ki, w_wi, w_gf, w_gd) = _split_w_in(p["w_in"])
    cast = lambda a: a.astype(MM_DTYPE)
    pad_m = jnp.zeros((D, LANES - IDX_HEADS - FOX_HEADS), F32)
    wf = cast(jnp.concatenate([w_qf, w_kf, w_vf], axis=1))
    wd = cast(jnp.concatenate([w_qd, w_kd, w_vd], axis=1))
    wi = cast(jnp.concatenate([w_qi, w_ki, w_ki], axis=1))
    wm = cast(jnp.concatenate([w_wi, w_fl, pad_m], axis=1))
    wg = cast(jnp.concatenate([w_gf, w_gd], axis=1))
    bm = jnp.zeros((1, LANES), F32).at[0, MISC_F0:MISC_F0 + FOX_HEADS].set(p["b_forget"].astype(F32))
    bg = p["b_gate"].astype(F32).reshape(1, 2 * D)
    row = lambda a: a.astype(F32).reshape(1, D)

    qk_f, vt_f, qk_d, vt_d, qk_i, misc, gates = _input_stage(
        h, row(p["norm_mix_pre"]), wf, wd, wi, wm, wg, bm, bg, tabs, tm=tm_in)

    o_f, o_d = _token_mixers(qk_f, vt_f, misc, qk_i, qk_d, vt_d, tq=tq_mix,
                             topk=min(TOPK_MAX, S // 4))

    N = B * S
    h1 = _merge_stage(o_f.reshape(N, FOX_W), o_d.reshape(N, DSA_W), gates.reshape(N, 2 * D),
                      h.reshape(N, D), cast(p["w_branch_fox"]), cast(p["w_branch_dsa"]),
                      cast(p["w_out"]), row(p["norm_mix_post"]), tm=tm_merge)
    h2 = _ffn_stage(h1, row(p["norm_ffn_pre"]), cast(p["w_ffn_gate"]), cast(p["w_ffn_up"]),
                    cast(p["w_ffn_down"]), row(p["norm_ffn_post"]), tm=tm_ffn, tf=tf)
    return h2.reshape(B, S, D)


def kernel(x, norm_mix_pre, w_in, b_forget, b_gate, w_branch_fox, w_branch_dsa, w_out,
           norm_mix_post, norm_ffn_pre, w_ffn_gate, w_ffn_up, w_ffn_down, norm_ffn_post):
    B, S, D = x.shape
    params = dict(norm_mix_pre=norm_mix_pre, w_in=w_in, b_forget=b_forget, b_gate=b_gate,
                  w_branch_fox=w_branch_fox, w_branch_dsa=w_branch_dsa, w_out=w_out,
                  norm_mix_post=norm_mix_post, norm_ffn_pre=norm_ffn_pre, w_ffn_gate=w_ffn_gate,
                  w_ffn_up=w_ffn_up, w_ffn_down=w_ffn_down, norm_ffn_post=norm_ffn_post)
    tabs = (_rope_tables(S, HEAD_DIM // ROT_FRAC_DEN, HEAD_DIM)
            + _rope_tables(S, IDX_DIM // ROT_FRAC_DEN, IDX_DIM))
    tiles = dict(tm_in=min(512, S), tq_mix=256,
                 tm_merge=min(512, B * S), tm_ffn=min(512, B * S), tf=D_FF)
    h = x
    for l in range(w_in.shape[0]):
        h = _layer(h, {k: v[l] for k, v in params.items()}, tabs, **tiles)
    return h
```

```python
import functools

import jax
import jax.numpy as jnp
from jax import lax
from jax.experimental import pallas as pl
from jax.experimental.pallas import tpu as pltpu

D_MODEL = 1024
HEAD_DIM = 128
FOX_HEADS = 4
DSA_HEADS = 4
IDX_HEADS = 8
IDX_DIM = 64
ROT_FRAC_DEN = 4
ROPE_THETA = 500000.0
TOPK_MAX = 256
D_FF = 2816
RMS_EPS = 1e-6
FOX_W = FOX_HEADS * HEAD_DIM
DSA_W = DSA_HEADS * HEAD_DIM
IDX_W = IDX_HEADS * IDX_DIM
IN_WIDTHS = (FOX_W, FOX_W, FOX_W, FOX_HEADS, DSA_W, HEAD_DIM, HEAD_DIM,
             IDX_W, IDX_DIM, IDX_HEADS, D_MODEL, D_MODEL)

LANES = 128
SUBLANES = 8
VMEM_LIMIT = 56 * 1024 * 1024
MM_DTYPE = jnp.bfloat16
NEG_BIG = -1e30
LOG2E = 1.4426950408889634
Q_SCALE = HEAD_DIM ** -0.5 * LOG2E
MIX_BLOCK = 256

F32 = jnp.float32


def _nt_dot(a, b):
    return lax.dot_general(a, b, (((1,), (1,)), ((), ())), preferred_element_type=F32)


def _dot(a, b):
    return jnp.dot(a, b, preferred_element_type=F32)


def _rms(x, g):
    return x * lax.rsqrt(jnp.mean(x * x, axis=-1, keepdims=True) + RMS_EPS) * g


def _sigmoid(z):
    return 1.0 / (1.0 + jnp.exp(-z))


def _sublane_fold(x, op):
    acc = x[0:SUBLANES, :]
    for j in range(1, x.shape[0] // SUBLANES):
        acc = op(acc, x[j * SUBLANES:(j + 1) * SUBLANES, :])
    return acc


MISC_F0 = IDX_HEADS


def _rope(seg, cos, sin_a, sin_b, half):
    return (seg * cos
            + pltpu.roll(seg, LANES - half, axis=1) * sin_a
            + pltpu.roll(seg, half, axis=1) * sin_b)


def _input_kernel(x_ref, g_ref, wf_ref, wd_ref, wi_ref, wm_ref, wg_ref, bm_ref, bg_ref,
                  cd_ref, sad_ref, sbd_ref, ci_ref, sai_ref, sbi_ref,
                  qf_ref, kf_ref, vtf_ref, d_ref, vtd_ref, i_ref, m_ref, gate_ref, carry_ref,
                  *, idx_scale):
    tm = x_ref.shape[0]
    u = _rms(x_ref[...], g_ref[...]).astype(MM_DTYPE)

    def put_transposed(dst, v):
        v_t = v.T
        for cc in range(tm // MIX_BLOCK):
            dst[cc] = v_t[:, cc * MIX_BLOCK:(cc + 1) * MIX_BLOCK].astype(dst.dtype)

    yf = _dot(u, wf_ref[...])
    qf_ref[...] = (yf[:, 0:FOX_W] * Q_SCALE).astype(qf_ref.dtype)
    for h in range(FOX_HEADS):
        kf_ref[h] = yf[:, FOX_W + h * HEAD_DIM:FOX_W + (h + 1) * HEAD_DIM].astype(kf_ref.dtype)
        put_transposed(vtf_ref.at[h], yf[:, 2 * FOX_W + h * HEAD_DIM:2 * FOX_W + (h + 1) * HEAD_DIM])

    yd = _dot(u, wd_ref[...])
    cd, sad, sbd = cd_ref[...], sad_ref[...], sbd_ref[...]
    for j in range(DSA_HEADS + 1):
        seg = _rope(yd[:, j * LANES:(j + 1) * LANES], cd, sad, sbd, HEAD_DIM // ROT_FRAC_DEN // 2)
        if j < DSA_HEADS:
            seg = seg * Q_SCALE
        d_ref[:, j * LANES:(j + 1) * LANES] = seg.astype(d_ref.dtype)
    put_transposed(vtd_ref, yd[:, (DSA_HEADS + 1) * LANES:])

    yi = _dot(u, wi_ref[...])
    ci, sai, sbi = ci_ref[...], sai_ref[...], sbi_ref[...]
    for j in range(yi.shape[1] // LANES):
        seg = _rope(yi[:, j * LANES:(j + 1) * LANES], ci, sai, sbi, IDX_DIM // ROT_FRAC_DEN // 2)
        i_ref[:, j * LANES:(j + 1) * LANES] = seg.astype(i_ref.dtype)

    gate_ref[...] = _sigmoid(_dot(u, wg_ref[...]) + bg_ref[...]).astype(gate_ref.dtype)

    ym = _dot(u, wm_ref[...])
    col = lax.broadcasted_iota(jnp.int32, ym.shape, 1)
    row = lax.broadcasted_iota(jnp.int32, ym.shape, 0)
    is_f = (col >= MISC_F0) & (col < MISC_F0 + FOX_HEADS)
    z = ym + bm_ref[...]
    log_f = jnp.where(is_f, jnp.minimum(z, 0.0) - jnp.log1p(jnp.exp(-jnp.abs(z))), 0.0)

    @pl.when(pl.program_id(1) == 0)
    def _():
        carry_ref[...] = jnp.zeros_like(carry_ref)

    c = log_f
    shift = 1
    while shift < tm:
        c = c + jnp.where(row >= shift, pltpu.roll(c, shift, axis=0), 0.0)
        shift *= 2
    c = c + carry_ref[0:1, :]
    carry_ref[0:1, :] = c[tm - 1:tm, :]
    m_ref[...] = jnp.where(is_f, c * LOG2E, ym * idx_scale)


def _input_stage(x, gain, wf, wd, wi, wm, wg, bm, bg, tabs, *, tm):
    B, S, D = x.shape
    idx_scale = (IDX_DIM ** -0.5) * (IDX_HEADS ** -0.5)
    full = lambda a: pl.BlockSpec(a.shape, lambda b, t: (0,) * a.ndim)
    tab = pl.BlockSpec((tm, LANES), lambda b, t: (t, 0))
    rows = lambda w: pl.BlockSpec((None, tm, w), lambda b, t: (b, t, 0))
    qk_d_w = DSA_W + HEAD_DIM
    H = FOX_HEADS
    cpb = tm // MIX_BLOCK
    n_chunks = S // MIX_BLOCK
    chunk_tile = (HEAD_DIM, MIX_BLOCK)
    return pl.pallas_call(
        functools.partial(_input_kernel, idx_scale=idx_scale),
        grid=(B, S // tm),
        in_specs=[rows(D), full(gain), full(wf), full(wd), full(wi), full(wm), full(wg),
                  full(bm), full(bg)] + [tab] * 6,
        out_specs=[rows(FOX_W),
                   pl.BlockSpec((None, H, tm, HEAD_DIM), lambda b, t: (b, 0, t, 0)),
                   pl.BlockSpec((None, H, cpb) + chunk_tile, lambda b, t: (b, 0, t, 0, 0)),
                   rows(qk_d_w),
                   pl.BlockSpec((None, cpb) + chunk_tile, lambda b, t: (b, t, 0, 0)),
                   rows(wi.shape[1]), rows(wm.shape[1]), rows(wg.shape[1])],
        out_shape=[jax.ShapeDtypeStruct((B, S, FOX_W), MM_DTYPE),
                   jax.ShapeDtypeStruct((B, H, S, HEAD_DIM), MM_DTYPE),
                   jax.ShapeDtypeStruct((B, H, n_chunks) + chunk_tile, MM_DTYPE),
                   jax.ShapeDtypeStruct((B, S, qk_d_w), MM_DTYPE),
                   jax.ShapeDtypeStruct((B, n_chunks) + chunk_tile, MM_DTYPE),
                   jax.ShapeDtypeStruct((B, S, wi.shape[1]), MM_DTYPE),
                   jax.ShapeDtypeStruct((B, S, wm.shape[1]), F32),
                   jax.ShapeDtypeStruct((B, S, wg.shape[1]), MM_DTYPE)],
        scratch_shapes=[pltpu.VMEM((8, LANES), F32)],
        compiler_params=pltpu.CompilerParams(
            dimension_semantics=("arbitrary", "arbitrary"), vmem_limit_bytes=VMEM_LIMIT),
        name="input_stage",
    )(x, gain, wf, wd, wi, wm, wg, bm, bg, *tabs)


def _flash_attend(i, qk_fn, mask_fn, pv_fn, o_ref, scratch, heads, tq, side_fn=None):
    s_buf, p_buf, m_s, l_s, a_s, acc_s = scratch
    n_t = tq // LANES
    head_cols = [slice(h * tq, (h + 1) * tq) for h in range(heads)]
    m_s[...] = jnp.full(m_s.shape, NEG_BIG, F32)
    l_s[...] = jnp.zeros(l_s.shape, F32)
    a_s[...] = jnp.ones(a_s.shape, F32)
    acc_s[...] = jnp.zeros(acc_s.shape, F32)
    p_buf[1] = jnp.zeros(p_buf.shape[1:], p_buf.dtype)

    def put_logits(slot, c):
        for h in range(heads):
            res = qk_fn(c, h)
            for t in range(n_t):
                s_buf[slot, h * n_t + t] = res[:, t * LANES:(t + 1) * LANES]

    def lagged_values(c, slot):
        for h, hc in enumerate(head_cols):
            p = jnp.concatenate([p_buf[slot, h * n_t + t] for t in range(n_t)], axis=1)
            acc_s[h] = a_s[0:1, hc] * acc_s[h] + pv_fn(c, h, p)

    put_logits(0, 0)

    def stage(c, cur, last):
        lagged_values(jnp.maximum(c - 1, 0), 1 - cur)
        if not last:
            put_logits(1 - cur, c + 1)
        if side_fn is not None:
            side_fn(c, last)
        for j in range(heads * n_t):
            cols = slice(j * LANES, (j + 1) * LANES)
            s = mask_fn(c, s_buf[cur, j], last, j % n_t)
            m_old = m_s[0:1, cols]
            m_new = jnp.maximum(m_old, jnp.max(s, axis=0, keepdims=True))
            alpha = jnp.exp2(m_old - m_new)
            p = jnp.exp2(s - m_new)
            l_s[0:1, cols] = alpha * l_s[0:1, cols] + jnp.sum(p, axis=0, keepdims=True)
            p_buf[cur, j] = p.astype(p_buf.dtype)
            a_s[0:1, cols] = alpha
            m_s[0:1, cols] = m_new

    def pair(j, _):
        stage(2 * j, 0, False)
        stage(2 * j + 1, 1, False)
        return 0

    lax.fori_loop(0, i // 2, pair, 0)

    @pl.when(i % 2 == 0)
    def _():
        stage(i, 0, True)
        lagged_values(i, 0)

    @pl.when(i % 2 == 1)
    def _():
        stage(i - 1, 0, False)
        stage(i, 1, True)
        lagged_values(i, 1)

    for h, hc in enumerate(head_cols):
        o_t = acc_s[h] / l_s[0:1, hc]
        o_ref[:, h * HEAD_DIM:(h + 1) * HEAD_DIM] = o_t.T.astype(o_ref.dtype)


def _flash_scratch(heads, tq):
    w = heads * tq
    tiles = w // LANES
    return [pltpu.VMEM((2, tiles, tq, LANES), F32),
            pltpu.VMEM((2, tiles, tq, LANES), MM_DTYPE),
            pltpu.VMEM((SUBLANES, w), F32),
            pltpu.VMEM((SUBLANES, w), F32),
            pltpu.VMEM((SUBLANES, w), F32),
            pltpu.VMEM((heads, HEAD_DIM, tq), F32)]


def _causal_tiles(ch, tq):
    key_off = lax.broadcasted_iota(jnp.int32, (ch, LANES), 0)
    q_off = lax.broadcasted_iota(jnp.int32, (ch, LANES), 1)
    return [key_off <= q_off + t * LANES for t in range(tq // LANES)]


def _fox_fns(q_ref, k_ref, vt_ref, m_ref, tq):
    ch = tq
    causal = _causal_tiles(ch, tq)

    def qk(c, h):
        start = pl.multiple_of(c * ch, ch)
        s = _nt_dot(k_ref[h, pl.ds(start, ch), :],
                    q_ref[:, h * HEAD_DIM:(h + 1) * HEAD_DIM])
        return s - m_ref[pl.ds(start, ch), MISC_F0 + h:MISC_F0 + h + 1]

    def mask(c, s, last, t):
        return jnp.where(causal[t], s, NEG_BIG) if last else s

    def pv(c, h, p):
        return _dot(vt_ref[h, c], p)

    return qk, mask, pv


IDX_BITS = 13
SEARCH_MAX_ITERS = 256
SEARCH_FIXED_ITERS = 12
F32_TINY = 1.1754943508222875e-38


def _mixers_kernel(qf_ref, kf_ref, vtf_ref, m_ref, qi_ref, ki_ref, qd_ref, kd_ref, vtd_ref,
                   of_ref, od_ref, sc_ref, ext_ref, *flash_scratch, topk):
    i = pl.program_id(1)
    tq = qi_ref.shape[0]
    ch = tq
    n_ch = i + 1
    key_off = lax.broadcasted_iota(jnp.int32, (ch, tq), 0)
    q_off = lax.broadcasted_iota(jnp.int32, (ch, tq), 1)
    causal_diag = key_off <= q_off
    causal_tiles = _causal_tiles(ch, tq)
    tiles = [slice(t * LANES, (t + 1) * LANES) for t in range(tq // LANES)]
    key_off_tile = key_off[:, 0:LANES]

    half_lane = lax.broadcasted_iota(jnp.int32, (tq, LANES), 1) // IDX_DIM
    q_heads = []
    for h in range(IDX_HEADS):
        pair = qi_ref[:, (h // 2) * LANES:(h // 2 + 1) * LANES]
        q_heads.append(jnp.where(half_lane == h % 2, pair, jnp.zeros_like(pair)))
    w_t = m_ref[pl.ds(pl.multiple_of(i * tq, tq), tq), :].T
    ext_ref[0:SUBLANES, :] = jnp.full((SUBLANES, tq), -jnp.inf, F32)
    ext_ref[SUBLANES:2 * SUBLANES, :] = jnp.full((SUBLANES, tq), jnp.inf, F32)

    def score_chunk(c, diag):
        start = pl.multiple_of(c * ch, ch)
        kc = ki_ref[pl.ds(start, ch), :]
        acc = jnp.zeros((ch, tq), F32)
        for h in range(IDX_HEADS):
            acc = acc + jnp.maximum(_nt_dot(kc, q_heads[h]), 0.0) * w_t[h:h + 1, :]
        lo_src = acc
        if diag:
            lo_src = jnp.where(causal_diag, acc, jnp.inf)
            acc = jnp.where(causal_diag, acc, -jnp.inf)
        for t, lanes in enumerate(tiles):
            sc_ref[t, pl.ds(start, ch), :] = acc[:, lanes]
        ext_ref[0:SUBLANES, :] = jnp.maximum(ext_ref[0:SUBLANES, :],
                                             _sublane_fold(acc, jnp.maximum))
        ext_ref[SUBLANES:2 * SUBLANES, :] = jnp.minimum(ext_ref[SUBLANES:2 * SUBLANES, :],
                                                        _sublane_fold(lo_src, jnp.minimum))

    fox_qk, fox_mask, fox_pv = _fox_fns(qf_ref, kf_ref, vtf_ref, m_ref, tq)
    _flash_attend(i, fox_qk, fox_mask, fox_pv, of_ref, flash_scratch, FOX_HEADS, tq,
                  side_fn=score_chunk)

    row_max = jnp.max(ext_ref[0:SUBLANES, :], axis=0, keepdims=True)
    row_min = jnp.min(ext_ref[SUBLANES:2 * SUBLANES, :], axis=0, keepdims=True)

    def count(pred):
        def hits(c):
            start = pl.multiple_of(c * ch, ch)
            return jnp.concatenate(
                [_sublane_fold(jnp.where(pred(sc_ref[t, pl.ds(start, ch), :], start, lanes), 1.0, 0.0),
                               jnp.add) for t, lanes in enumerate(tiles)], axis=1)
        zero = jnp.zeros((SUBLANES, tq), F32)
        part = lax.fori_loop(0, n_ch // 2, lambda j, cnt: cnt + hits(2 * j) + hits(2 * j + 1), zero)
        part = part + lax.cond(n_ch % 2 == 1, lambda: hits(n_ch - 1), lambda: zero)
        return jnp.sum(part, axis=0, keepdims=True)

    kf = jnp.float32(topk)
    n_causal = (i * tq + 1 + lax.broadcasted_iota(jnp.int32, (1, tq), 1)).astype(F32)
    few = n_causal < kf

    def active(lo, hi, c_lo):
        mid = 0.5 * lo + 0.5 * hi
        return (c_lo > kf) & (mid > lo) & (mid < hi)

    def any_active(state):
        lo, hi, c_lo, it = state[0], state[1], state[2], state[-1]
        busy = jnp.max(jnp.where(active(lo, hi, c_lo), 1.0, 0.0))
        return (busy > 0.5) & (it < SEARCH_MAX_ITERS)

    log_k = jnp.log(kf - 0.5)

    def narrow(state):
        lo, hi, c_lo, c_hi, w_lo, w_hi, last, it = state
        act = active(lo, hi, c_lo)
        mid = 0.5 * lo + 0.5 * hi
        f_lo = (jnp.log(c_lo) - log_k) * w_lo
        f_hi = (log_k - jnp.log(jnp.maximum(c_hi, 0.25))) * w_hi
        guess = lo + (hi - lo) * (f_lo / (f_lo + f_hi))
        cand = jnp.where((guess > lo) & (guess < hi), guess, mid)
        cand = jnp.where((lo < 0.0) & (hi > 0.0), 0.0, cand)
        cand = jnp.where((lo == 0.0) & (hi > F32_TINY), F32_TINY, cand)
        cnt = count(lambda s, _, lanes: s >= cand[:, lanes])
        up = act & (cnt >= kf)
        dn = act & (cnt < kf)
        w_lo = jnp.where(up, 1.0, jnp.where(dn, jnp.where(last < 0.0, 0.5 * w_lo, 1.0), w_lo))
        w_hi = jnp.where(dn, 1.0, jnp.where(up, jnp.where(last > 0.0, 0.5 * w_hi, 1.0), w_hi))
        last = jnp.where(up, 1.0, jnp.where(dn, -1.0, last))
        return (jnp.where(up, cand, lo), jnp.where(dn, cand, hi),
                jnp.where(up, cnt, c_lo), jnp.where(dn, cnt, c_hi), w_lo, w_hi, last, it + 1)

    above_max = row_max + (row_max - row_min) + 1.0
    ones = jnp.ones((1, tq), F32)
    start_state = (row_min, above_max, jnp.where(few, kf, n_causal), jnp.zeros((1, tq), F32),
                   ones, ones, jnp.zeros((1, tq), F32), jnp.int32(0))
    state = lax.fori_loop(0, SEARCH_FIXED_ITERS, lambda _, st: narrow(st), start_state)
    lo, _, c_lo, c_hi = lax.while_loop(any_active, narrow, state)[:4]

    t = jnp.where(few, -jnp.inf, lo)

    need = kf - c_hi

    @pl.when(jnp.max(c_lo) > kf)
    def _():
        def idx_step(n, jt):
            cand = jt | (jnp.int32(1) << (IDX_BITS - 1 - n))
            below = count(lambda s, start, lanes:
                          (s == t[:, lanes]) & (start + key_off_tile < cand[:, lanes]))
            return jnp.where(below <= need, cand, jt)
        jt = lax.fori_loop(0, IDX_BITS, idx_step, jnp.zeros((1, tq), jnp.int32))
        jt = jnp.where(c_lo > kf, jt, 2 ** IDX_BITS - 1)

        def drop_surplus(c, _):
            start = pl.multiple_of(c * ch, ch)
            for tile, lanes in enumerate(tiles):
                s = sc_ref[tile, pl.ds(start, ch), :]
                surplus = (s == t[:, lanes]) & (start + key_off_tile >= jt[:, lanes])
                sc_ref[tile, pl.ds(start, ch), :] = jnp.where(surplus, -jnp.inf, s)
            return 0
        lax.fori_loop(0, n_ch, drop_surplus, 0)

    def qk(c, h):
        return _nt_dot(kd_ref[pl.ds(pl.multiple_of(c * ch, ch), ch), :],
                       qd_ref[:, h * HEAD_DIM:(h + 1) * HEAD_DIM])

    def mask(c, s, last, tile):
        sel = sc_ref[tile, pl.ds(pl.multiple_of(c * ch, ch), ch), :] >= t[:, tiles[tile]]
        if last:
            sel = sel & causal_tiles[tile]
        return jnp.where(sel, s, NEG_BIG)

    def pv(c, h, p):
        return _dot(vtd_ref[c], p)

    _flash_attend(i, qk, mask, pv, od_ref, flash_scratch, DSA_HEADS, tq)


def _token_mixers(q_f, k_f, vt_f, misc, qk_i, qk_d, vt_d, *, topk):
    B, S, _ = qk_d.shape
    tq = MIX_BLOCK
    assert FOX_HEADS == DSA_HEADS and FOX_W == DSA_W
    blk = lambda w, col: pl.BlockSpec((None, tq, w), lambda b, i: (b, i, col))
    res = lambda w, col: pl.BlockSpec((None, S, w), lambda b, i: (b, 0, col))
    whole = lambda a: pl.BlockSpec((None,) + a.shape[1:], lambda b, i: (b,) + (0,) * (a.ndim - 1))
    return pl.pallas_call(
        functools.partial(_mixers_kernel, topk=topk),
        grid=(B, S // tq),
        in_specs=[blk(FOX_W, 0), whole(k_f), whole(vt_f), res(LANES, 0),
                  blk(IDX_W, 0), res(LANES, IDX_W // LANES),
                  blk(DSA_W, 0), res(LANES, DSA_W // LANES), whole(vt_d)],
        out_specs=[blk(FOX_W, 0), blk(DSA_W, 0)],
        out_shape=[jax.ShapeDtypeStruct((B, S, FOX_W), MM_DTYPE),
                   jax.ShapeDtypeStruct((B, S, DSA_W), MM_DTYPE)],
        scratch_shapes=[pltpu.VMEM((tq // LANES, S, LANES), F32),
                        pltpu.VMEM((2 * SUBLANES, tq), F32)]
                       + _flash_scratch(DSA_HEADS, tq),
        compiler_params=pltpu.CompilerParams(
            dimension_semantics=("arbitrary", "arbitrary"), vmem_limit_bytes=VMEM_LIMIT),
        name="token_mixers",
    )(q_f, k_f, vt_f, misc, qk_i, qk_i, qk_d, qk_d, vt_d)


def _merge_kernel(of_ref, od_ref, gate_ref, x_ref, wbf_ref, wbd_ref, wo_ref, g_ref, h_ref):
    a = _dot(of_ref[...], wbf_ref[...])
    b = _dot(od_ref[...], wbd_ref[...])
    mixed = (gate_ref[:, 0:D_MODEL].astype(F32) * a
             + gate_ref[:, D_MODEL:2 * D_MODEL].astype(F32) * b)
    y = _dot(mixed.astype(MM_DTYPE), wo_ref[...])
    h_ref[...] = x_ref[...] + _rms(y, g_ref[...])


def _merge_stage(o_f, o_d, gates, x, wbf, wbd, wo, g_post, *, tm):
    N, D = x.shape
    row = lambda w: pl.BlockSpec((tm, w), lambda t: (t, 0))
    full = lambda a: pl.BlockSpec(a.shape, lambda t: (0,) * a.ndim)
    return pl.pallas_call(
        _merge_kernel,
        grid=(N // tm,),
        in_specs=[row(FOX_W), row(DSA_W), row(2 * D), row(D),
                  full(wbf), full(wbd), full(wo), full(g_post)],
        out_specs=row(D),
        out_shape=jax.ShapeDtypeStruct((N, D), F32),
        compiler_params=pltpu.CompilerParams(
            dimension_semantics=("arbitrary",), vmem_limit_bytes=VMEM_LIMIT),
        name="merge_stage",
    )(o_f, o_d, gates, x, wbf, wbd, wo, g_post)


def _ffn_kernel(h_ref, gpre_ref, wg_ref, wu_ref, wd_ref, gpost_ref, o_ref, v_ref, acc_ref):
    j = pl.program_id(1)

    @pl.when(j == 0)
    def _():
        v_ref[...] = _rms(h_ref[...], gpre_ref[...]).astype(v_ref.dtype)
        acc_ref[...] = jnp.zeros_like(acc_ref)

    v = v_ref[...]
    g = _dot(v, wg_ref[...])
    a = (g * _sigmoid(g) * _dot(v, wu_ref[...])).astype(MM_DTYPE)
    acc_ref[...] += _dot(a, wd_ref[...])

    @pl.when(j == pl.num_programs(1) - 1)
    def _():
        o_ref[...] = h_ref[...] + _rms(acc_ref[...], gpost_ref[...])


def _ffn_stage(h, g_pre, wg, wu, wd, g_post, *, tm, tf):
    N, D = h.shape
    F = wg.shape[1]
    full = lambda a: pl.BlockSpec(a.shape, lambda t, j: (0,) * a.ndim)
    return pl.pallas_call(
        _ffn_kernel,
        grid=(N // tm, F // tf),
        in_specs=[pl.BlockSpec((tm, D), lambda t, j: (t, 0)), full(g_pre),
                  pl.BlockSpec((D, tf), lambda t, j: (0, j)),
                  pl.BlockSpec((D, tf), lambda t, j: (0, j)),
                  pl.BlockSpec((tf, D), lambda t, j: (j, 0)), full(g_post)],
        out_specs=pl.BlockSpec((tm, D), lambda t, j: (t, 0)),
        out_shape=jax.ShapeDtypeStruct((N, D), F32),
        scratch_shapes=[pltpu.VMEM((tm, D), MM_DTYPE), pltpu.VMEM((tm, D), F32)],
        compiler_params=pltpu.CompilerParams(
            dimension_semantics=("arbitrary", "arbitrary"), vmem_limit_bytes=VMEM_LIMIT),
        name="ffn_stage",
    )(h, g_pre, wg, wu, wd, g_post)


def _rope_tables(S, rot, period):
    half = rot // 2
    inv_freq = jnp.float32(ROPE_THETA) ** (-jnp.arange(half, dtype=F32) * 2.0 / rot)
    ang = jnp.arange(S).astype(F32)[:, None] * inv_freq[None, :]
    cos, sin = jnp.cos(ang), jnp.sin(ang)
    one = jnp.ones((S, period - rot), F32)
    zero = jnp.zeros((S, period - rot), F32)
    zh = jnp.zeros((S, half), F32)
    reps = LANES // period
    cos_t = jnp.tile(jnp.concatenate([cos, cos, one], axis=1), (1, reps))
    sin_a = jnp.tile(jnp.concatenate([-sin, zh, zero], axis=1), (1, reps))
    sin_b = jnp.tile(jnp.concatenate([zh, sin, zero], axis=1), (1, reps))
    return cos_t, sin_a, sin_b


def _split_w_in(w):
    parts, off = [], 0
    for width in IN_WIDTHS:
        parts.append(w[:, off:off + width])
        off += width
    return parts


def _layer(h, p, tabs, *, tm_in, tm_merge, tm_ffn, tf):
    B, S, D = h.shape
    (w_qf, w_kf, w_vf, w_fl, w_qd, w_kd, w_vd, w_qi, w_ki, w_wi, w_gf, w_gd) = _split_w_in(p["w_in"])
    cast = lambda a: a.astype(MM_DTYPE)
    pad_m = jnp.zeros((D, LANES - IDX_HEADS - FOX_HEADS), F32)
    wf = cast(jnp.concatenate([w_qf, w_kf, w_vf], axis=1))
    wd = cast(jnp.concatenate([w_qd, w_kd, w_vd], axis=1))
    wi = cast(jnp.concatenate([w_qi, w_ki, w_ki], axis=1))
    wm = cast(jnp.concatenate([w_wi, w_fl, pad_m], axis=1))
    wg = cast(jnp.concatenate([w_gf, w_gd], axis=1))
    bm = jnp.zeros((1, LANES), F32).at[0, MISC_F0:MISC_F0 + FOX_HEADS].set(p["b_forget"].astype(F32))
    bg = p["b_gate"].astype(F32).reshape(1, 2 * D)
    row = lambda a: a.astype(F32).reshape(1, D)

    q_f, k_f, vt_f, qk_d, vt_d, qk_i, misc, gates = _input_stage(
        h, row(p["norm_mix_pre"]), wf, wd, wi, wm, wg, bm, bg, tabs, tm=tm_in)

    o_f, o_d = _token_mixers(q_f, k_f, vt_f, misc, qk_i, qk_d, vt_d, topk=min(TOPK_MAX, S // 4))

    N = B * S
    h1 = _merge_stage(o_f.reshape(N, FOX_W), o_d.reshape(N, DSA_W), gates.reshape(N, 2 * D),
                      h.reshape(N, D), cast(p["w_branch_fox"]), cast(p["w_branch_dsa"]),
                      cast(p["w_out"]), row(p["norm_mix_post"]), tm=tm_merge)
    h2 = _ffn_stage(h1, row(p["norm_ffn_pre"]), cast(p["w_ffn_gate"]), cast(p["w_ffn_up"]),
                    cast(p["w_ffn_down"]), row(p["norm_ffn_post"]), tm=tm_ffn, tf=tf)
    return h2.reshape(B, S, D)


def kernel(x, norm_mix_pre, w_in, b_forget, b_gate, w_branch_fox, w_branch_dsa, w_out,
           norm_mix_post, norm_ffn_pre, w_ffn_gate, w_ffn_up, w_ffn_down, norm_ffn_post):
    B, S, D = x.shape
    params = dict(norm_mix_pre=norm_mix_pre, w_in=w_in, b_forget=b_forget, b_gate=b_gate,
                  w_branch_fox=w_branch_fox, w_branch_dsa=w_branch_dsa, w_out=w_out,
                  norm_mix_post=norm_mix_post, norm_ffn_pre=norm_ffn_pre, w_ffn_gate=w_ffn_gate,
                  w_ffn_up=w_ffn_up, w_ffn_down=w_ffn_down, norm_ffn_post=norm_ffn_post)
    tabs = (_rope_tables(S, HEAD_DIM // ROT_FRAC_DEN, HEAD_DIM)
            + _rope_tables(S, IDX_DIM // ROT_FRAC_DEN, IDX_DIM))
    tiles = dict(tm_in=min(2 * MIX_BLOCK, S), tm_merge=min(512, B * S), tm_ffn=min(512, B * S), tf=D_FF)
    h = x
    for l in range(w_in.shape[0]):
        h = _layer(h, {k: v[l] for k, v in params.items()}, tabs, **tiles)
    return h
```

```python
import functools

import jax
import jax.numpy as jnp
from jax import lax
from jax.experimental import pallas as pl
from jax.experimental.pallas import tpu as pltpu

D_MODEL = 1024
HEAD_DIM = 128
FOX_HEADS = 4
DSA_HEADS = 4
IDX_HEADS = 8
IDX_DIM = 64
ROT_FRAC_DEN = 4
ROPE_THETA = 500000.0
TOPK_MAX = 256
D_FF = 2816
RMS_EPS = 1e-6
FOX_W = FOX_HEADS * HEAD_DIM
DSA_W = DSA_HEADS * HEAD_DIM
IDX_W = IDX_HEADS * IDX_DIM
IN_WIDTHS = (FOX_W, FOX_W, FOX_W, FOX_HEADS, DSA_W, HEAD_DIM, HEAD_DIM,
             IDX_W, IDX_DIM, IDX_HEADS, D_MODEL, D_MODEL)

LANES = 128
SUBLANES = 8
VMEM_LIMIT = 56 * 1024 * 1024
MM_DTYPE = jnp.bfloat16
NEG_BIG = -1e30
LOG2E = 1.4426950408889634
Q_SCALE = HEAD_DIM ** -0.5 * LOG2E

F32 = jnp.float32


def _nt_dot(a, b):
    return lax.dot_general(a, b, (((1,), (1,)), ((), ())), preferred_element_type=F32)


def _dot(a, b):
    return jnp.dot(a, b, preferred_element_type=F32)


def _rms(x, g):
    return x * lax.rsqrt(jnp.mean(x * x, axis=-1, keepdims=True) + RMS_EPS) * g


def _sigmoid(z):
    return 1.0 / (1.0 + jnp.exp(-z))


def _sublane_fold(x, op):
    acc = x[0:SUBLANES, :]
    for j in range(1, x.shape[0] // SUBLANES):
        acc = op(acc, x[j * SUBLANES:(j + 1) * SUBLANES, :])
    return acc


MISC_F0 = IDX_HEADS


def _rope(seg, cos, sin_a, sin_b, half):
    return (seg * cos
            + pltpu.roll(seg, LANES - half, axis=1) * sin_a
            + pltpu.roll(seg, half, axis=1) * sin_b)


def _input_kernel(x_ref, g_ref, wf_ref, wd_ref, wi_ref, wm_ref, wg_ref, bm_ref, bg_ref,
                  cd_ref, sad_ref, sbd_ref, ci_ref, sai_ref, sbi_ref,
                  f_ref, vtf_ref, d_ref, vtd_ref, i_ref, m_ref, gate_ref, carry_ref, *, idx_scale):
    tm = x_ref.shape[0]
    u = _rms(x_ref[...], g_ref[...]).astype(MM_DTYPE)

    yf = _dot(u, wf_ref[...])
    f_ref[:, 0:FOX_W] = (yf[:, 0:FOX_W] * Q_SCALE).astype(f_ref.dtype)
    f_ref[:, FOX_W:2 * FOX_W] = yf[:, FOX_W:2 * FOX_W].astype(f_ref.dtype)
    for h in range(FOX_HEADS):
        v = yf[:, 2 * FOX_W + h * HEAD_DIM:2 * FOX_W + (h + 1) * HEAD_DIM]
        vtf_ref[h * HEAD_DIM:(h + 1) * HEAD_DIM, :] = v.T.astype(vtf_ref.dtype)

    yd = _dot(u, wd_ref[...])
    cd, sad, sbd = cd_ref[...], sad_ref[...], sbd_ref[...]
    for j in range(DSA_HEADS + 1):
        seg = _rope(yd[:, j * LANES:(j + 1) * LANES], cd, sad, sbd, HEAD_DIM // ROT_FRAC_DEN // 2)
        if j < DSA_HEADS:
            seg = seg * Q_SCALE
        d_ref[:, j * LANES:(j + 1) * LANES] = seg.astype(d_ref.dtype)
    vtd_ref[...] = yd[:, (DSA_HEADS + 1) * LANES:].T.astype(vtd_ref.dtype)

    yi = _dot(u, wi_ref[...])
    ci, sai, sbi = ci_ref[...], sai_ref[...], sbi_ref[...]
    for j in range(yi.shape[1] // LANES):
        seg = _rope(yi[:, j * LANES:(j + 1) * LANES], ci, sai, sbi, IDX_DIM // ROT_FRAC_DEN // 2)
        i_ref[:, j * LANES:(j + 1) * LANES] = seg.astype(i_ref.dtype)

    gate_ref[...] = _sigmoid(_dot(u, wg_ref[...]) + bg_ref[...]).astype(gate_ref.dtype)

    ym = _dot(u, wm_ref[...])
    col = lax.broadcasted_iota(jnp.int32, ym.shape, 1)
    row = lax.broadcasted_iota(jnp.int32, ym.shape, 0)
    is_f = (col >= MISC_F0) & (col < MISC_F0 + FOX_HEADS)
    z = ym + bm_ref[...]
    log_f = jnp.where(is_f, jnp.minimum(z, 0.0) - jnp.log1p(jnp.exp(-jnp.abs(z))), 0.0)

    @pl.when(pl.program_id(1) == 0)
    def _():
        carry_ref[...] = jnp.zeros_like(carry_ref)

    c = log_f
    shift = 1
    while shift < tm:
        c = c + jnp.where(row >= shift, pltpu.roll(c, shift, axis=0), 0.0)
        shift *= 2
    c = c + carry_ref[0:1, :]
    carry_ref[0:1, :] = c[tm - 1:tm, :]
    m_ref[...] = jnp.where(is_f, c * LOG2E, ym * idx_scale)


def _input_stage(x, gain, wf, wd, wi, wm, wg, bm, bg, tabs, *, tm):
    B, S, D = x.shape
    idx_scale = (IDX_DIM ** -0.5) * (IDX_HEADS ** -0.5)
    full = lambda a: pl.BlockSpec(a.shape, lambda b, t: (0,) * a.ndim)
    tab = pl.BlockSpec((tm, LANES), lambda b, t: (t, 0))
    rows = lambda w: pl.BlockSpec((None, tm, w), lambda b, t: (b, t, 0))
    cols = lambda h: pl.BlockSpec((None, h, tm), lambda b, t: (b, 0, t))
    qk_d_w = DSA_W + HEAD_DIM
    return pl.pallas_call(
        functools.partial(_input_kernel, idx_scale=idx_scale),
        grid=(B, S // tm),
        in_specs=[rows(D), full(gain), full(wf), full(wd), full(wi), full(wm), full(wg),
                  full(bm), full(bg)] + [tab] * 6,
        out_specs=[rows(2 * FOX_W), cols(FOX_W), rows(qk_d_w), cols(HEAD_DIM),
                   rows(wi.shape[1]), rows(wm.shape[1]), rows(wg.shape[1])],
        out_shape=[jax.ShapeDtypeStruct((B, S, 2 * FOX_W), MM_DTYPE),
                   jax.ShapeDtypeStruct((B, FOX_W, S), MM_DTYPE),
                   jax.ShapeDtypeStruct((B, S, qk_d_w), MM_DTYPE),
                   jax.ShapeDtypeStruct((B, HEAD_DIM, S), MM_DTYPE),
                   jax.ShapeDtypeStruct((B, S, wi.shape[1]), MM_DTYPE),
                   jax.ShapeDtypeStruct((B, S, wm.shape[1]), F32),
                   jax.ShapeDtypeStruct((B, S, wg.shape[1]), MM_DTYPE)],
        scratch_shapes=[pltpu.VMEM((8, LANES), F32)],
        compiler_params=pltpu.CompilerParams(
            dimension_semantics=("arbitrary", "arbitrary"), vmem_limit_bytes=VMEM_LIMIT),
        name="input_stage",
    )(x, gain, wf, wd, wi, wm, wg, bm, bg, *tabs)


def _flash_attend(i, qk_fn, mask_fn, pv_fn, o_ref, scratch, heads, tq, side_fn=None):
    s_buf, p_buf, m_s, l_s, a_s, acc_s = scratch
    m_s[...] = jnp.full(m_s.shape, NEG_BIG, F32)
    l_s[...] = jnp.zeros(l_s.shape, F32)
    a_s[...] = jnp.ones(a_s.shape, F32)
    acc_s[...] = jnp.zeros(acc_s.shape, F32)
    p_buf[1] = jnp.zeros(p_buf.shape[1:], p_buf.dtype)
    s_buf[0] = qk_fn(0)

    def lagged_values(c, slot):
        acc_s[...] = a_s[0:1, :] * acc_s[...] + pv_fn(c, p_buf[slot])

    def stage(c, cur, last):
        lagged_values(jnp.maximum(c - 1, 0), 1 - cur)
        if not last:
            s_buf[1 - cur] = qk_fn(c + 1)
        if side_fn is not None:
            side_fn(c, last)
        s = mask_fn(c, s_buf[cur], last)
        m_old = m_s[0:1, :]
        m_new = jnp.maximum(m_old, jnp.max(s, axis=0, keepdims=True))
        alpha = jnp.exp2(m_old - m_new)
        p = jnp.exp2(s - m_new)
        l_s[0:1, :] = alpha * l_s[0:1, :] + jnp.sum(p, axis=0, keepdims=True)
        p_buf[cur] = p.astype(p_buf.dtype)
        a_s[0:1, :] = alpha
        m_s[0:1, :] = m_new

    def pair(j, _):
        stage(2 * j, 0, False)
        stage(2 * j + 1, 1, False)
        return 0

    lax.fori_loop(0, i // 2, pair, 0)

    @pl.when(i % 2 == 0)
    def _():
        stage(i, 0, True)
        lagged_values(i, 0)

    @pl.when(i % 2 == 1)
    def _():
        stage(i - 1, 0, False)
        stage(i, 1, True)
        lagged_values(i, 1)

    for h in range(heads):
        cols = slice(h * tq, (h + 1) * tq)
        o_t = acc_s[:, cols] / l_s[0:1, cols]
        o_ref[:, h * HEAD_DIM:(h + 1) * HEAD_DIM] = o_t.T.astype(o_ref.dtype)


def _flash_scratch(heads, tq):
    w = heads * tq
    return [pltpu.VMEM((2, tq, w), F32),
            pltpu.VMEM((2, tq, w), MM_DTYPE),
            pltpu.VMEM((SUBLANES, w), F32),
            pltpu.VMEM((SUBLANES, w), F32),
            pltpu.VMEM((SUBLANES, w), F32),
            pltpu.VMEM((HEAD_DIM, w), F32)]


def _fox_fns(q_ref, k_ref, vt_ref, m_ref, tq):
    ch = tq
    H = FOX_HEADS
    causal_diag = (lax.broadcasted_iota(jnp.int32, (ch, H * tq), 0)
                   <= lax.broadcasted_iota(jnp.int32, (ch, H * tq), 1) % tq)

    def qk(c):
        start = pl.multiple_of(c * ch, ch)
        parts = []
        for h in range(H):
            hd = slice(h * HEAD_DIM, (h + 1) * HEAD_DIM)
            s = _nt_dot(k_ref[pl.ds(start, ch), hd], q_ref[:, hd])
            parts.append(s - m_ref[pl.ds(start, ch), MISC_F0 + h:MISC_F0 + h + 1])
        return jnp.concatenate(parts, axis=1)

    def mask(c, s, last):
        return jnp.where(causal_diag, s, NEG_BIG) if last else s

    def pv(c, p):
        start = pl.multiple_of(c * ch, ch)
        return jnp.concatenate(
            [_dot(vt_ref[h * HEAD_DIM:(h + 1) * HEAD_DIM, pl.ds(start, ch)],
                  p[:, h * tq:(h + 1) * tq]) for h in range(H)], axis=1)

    return qk, mask, pv


IDX_BITS = 13
SEARCH_MAX_ITERS = 256
SEARCH_FIXED_ITERS = 14
F32_TINY = 1.1754943508222875e-38


def _mixers_kernel(qf_ref, kf_ref, vtf_ref, m_ref, qi_ref, ki_ref, qd_ref, kd_ref, vtd_ref,
                   of_ref, od_ref, sc_ref, ext_ref, *flash_scratch, topk):
    i = pl.program_id(1)
    tq = qi_ref.shape[0]
    ch = tq
    n_ch = i + 1
    key_off = lax.broadcasted_iota(jnp.int32, (ch, tq), 0)
    q_off = lax.broadcasted_iota(jnp.int32, (ch, tq), 1)
    causal_diag = key_off <= q_off

    half_lane = lax.broadcasted_iota(jnp.int32, (tq, LANES), 1) // IDX_DIM
    q_heads = []
    for h in range(IDX_HEADS):
        pair = qi_ref[:, (h // 2) * LANES:(h // 2 + 1) * LANES]
        q_heads.append(jnp.where(half_lane == h % 2, pair, jnp.zeros_like(pair)))
    w_t = m_ref[pl.ds(pl.multiple_of(i * tq, tq), tq), :].T
    ext_ref[0:SUBLANES, :] = jnp.full((SUBLANES, tq), -jnp.inf, F32)
    ext_ref[SUBLANES:2 * SUBLANES, :] = jnp.full((SUBLANES, tq), jnp.inf, F32)

    def score_chunk(c, diag):
        start = pl.multiple_of(c * ch, ch)
        kc = ki_ref[pl.ds(start, ch), :]
        acc = jnp.zeros((ch, tq), F32)
        for h in range(IDX_HEADS):
            acc = acc + jnp.maximum(_nt_dot(kc, q_heads[h]), 0.0) * w_t[h:h + 1, :]
        lo_src = acc
        if diag:
            lo_src = jnp.where(causal_diag, acc, jnp.inf)
            acc = jnp.where(causal_diag, acc, -jnp.inf)
        sc_ref[pl.ds(start, ch), :] = acc
        ext_ref[0:SUBLANES, :] = jnp.maximum(ext_ref[0:SUBLANES, :],
                                             _sublane_fold(acc, jnp.maximum))
        ext_ref[SUBLANES:2 * SUBLANES, :] = jnp.minimum(ext_ref[SUBLANES:2 * SUBLANES, :],
                                                        _sublane_fold(lo_src, jnp.minimum))

    fox_qk, fox_mask, fox_pv = _fox_fns(qf_ref, kf_ref, vtf_ref, m_ref, tq)
    _flash_attend(i, fox_qk, fox_mask, fox_pv, of_ref, flash_scratch, FOX_HEADS, tq,
                  side_fn=score_chunk)

    row_max = jnp.max(ext_ref[0:SUBLANES, :], axis=0, keepdims=True)
    row_min = jnp.min(ext_ref[SUBLANES:2 * SUBLANES, :], axis=0, keepdims=True)

    def count(pred):
        def hits(c):
            start = pl.multiple_of(c * ch, ch)
            hit = pred(sc_ref[pl.ds(start, ch), :], start)
            return _sublane_fold(jnp.where(hit, 1.0, 0.0), jnp.add)
        zero = jnp.zeros((SUBLANES, tq), F32)
        part = lax.fori_loop(0, n_ch // 2, lambda j, cnt: cnt + hits(2 * j) + hits(2 * j + 1), zero)
        part = part + lax.cond(n_ch % 2 == 1, lambda: hits(n_ch - 1), lambda: zero)
        return jnp.sum(part, axis=0, keepdims=True)

    kf = jnp.float32(topk)
    n_causal = (i * tq + 1 + lax.broadcasted_iota(jnp.int32, (1, tq), 1)).astype(F32)
    few = n_causal < kf

    def active(lo, hi, c_lo):
        mid = 0.5 * lo + 0.5 * hi
        return (c_lo > kf) & (mid > lo) & (mid < hi)

    def any_active(state):
        lo, hi, c_lo, it = state[0], state[1], state[2], state[-1]
        busy = jnp.max(jnp.where(active(lo, hi, c_lo), 1.0, 0.0))
        return (busy > 0.5) & (it < SEARCH_MAX_ITERS)

    log_k = jnp.log(kf - 0.5)

    def narrow(state):
        lo, hi, c_lo, c_hi, w_lo, w_hi, last, it = state
        act = active(lo, hi, c_lo)
        mid = 0.5 * lo + 0.5 * hi
        f_lo = (jnp.log(c_lo) - log_k) * w_lo
        f_hi = (log_k - jnp.log(jnp.maximum(c_hi, 0.25))) * w_hi
        guess = lo + (hi - lo) * (f_lo / (f_lo + f_hi))
        cand = jnp.where((guess > lo) & (guess < hi), guess, mid)
        cand = jnp.where((lo < 0.0) & (hi > 0.0), 0.0, cand)
        cand = jnp.where((lo == 0.0) & (hi > F32_TINY), F32_TINY, cand)
        cb = jnp.broadcast_to(cand, (ch, tq))
        cnt = count(lambda s, _: s >= cb)
        up = act & (cnt >= kf)
        dn = act & (cnt < kf)
        w_lo = jnp.where(up, 1.0, jnp.where(dn, jnp.where(last < 0.0, 0.5 * w_lo, 1.0), w_lo))
        w_hi = jnp.where(dn, 1.0, jnp.where(up, jnp.where(last > 0.0, 0.5 * w_hi, 1.0), w_hi))
        last = jnp.where(up, 1.0, jnp.where(dn, -1.0, last))
        return (jnp.where(up, cand, lo), jnp.where(dn, cand, hi),
                jnp.where(up, cnt, c_lo), jnp.where(dn, cnt, c_hi), w_lo, w_hi, last, it + 1)

    above_max = row_max + (row_max - row_min) + 1.0
    ones = jnp.ones((1, tq), F32)
    start_state = (row_min, above_max, jnp.where(few, kf, n_causal), jnp.zeros((1, tq), F32),
                   ones, ones, jnp.zeros((1, tq), F32), jnp.int32(0))
    state = lax.fori_loop(0, SEARCH_FIXED_ITERS, lambda _, st: narrow(st), start_state)
    lo, _, c_lo, c_hi = lax.while_loop(any_active, narrow, state)[:4]

    t = jnp.where(few, -jnp.inf, lo)
    tb = jnp.broadcast_to(t, (ch, tq))

    need = kf - c_hi

    @pl.when(jnp.max(c_lo) > kf)
    def _():
        def idx_step(n, jt):
            cand = jt | (jnp.int32(1) << (IDX_BITS - 1 - n))
            cb = jnp.broadcast_to(cand, (ch, tq))
            below = count(lambda s, start: (s == tb) & (start + key_off < cb))
            return jnp.where(below <= need, cand, jt)
        jt = lax.fori_loop(0, IDX_BITS, idx_step, jnp.zeros((1, tq), jnp.int32))
        jt = jnp.where(c_lo > kf, jt, 2 ** IDX_BITS - 1)
        jb = jnp.broadcast_to(jt, (ch, tq))

        def drop_surplus(c, _):
            start = pl.multiple_of(c * ch, ch)
            s = sc_ref[pl.ds(start, ch), :]
            surplus = (s == tb) & (start + key_off >= jb)
            sc_ref[pl.ds(start, ch), :] = jnp.where(surplus, -jnp.inf, s)
            return 0
        lax.fori_loop(0, n_ch, drop_surplus, 0)

    q_all = jnp.concatenate([qd_ref[:, h * HEAD_DIM:(h + 1) * HEAD_DIM]
                             for h in range(DSA_HEADS)], axis=0)

    def qk(c):
        return _nt_dot(kd_ref[pl.ds(pl.multiple_of(c * ch, ch), ch), :], q_all)

    def mask(c, s, last):
        start = pl.multiple_of(c * ch, ch)
        sel = sc_ref[pl.ds(start, ch), :] >= tb
        if last:
            sel = sel & causal_diag
        return jnp.concatenate([jnp.where(sel, s[:, h * tq:(h + 1) * tq], NEG_BIG)
                                for h in range(DSA_HEADS)], axis=1)

    def pv(c, p):
        return _dot(vtd_ref[:, pl.ds(pl.multiple_of(c * ch, ch), ch)], p)

    _flash_attend(i, qk, mask, pv, od_ref, flash_scratch, DSA_HEADS, tq)


def _token_mixers(qk_f, vt_f, misc, qk_i, qk_d, vt_d, *, tq, topk):
    B, S, _ = qk_d.shape
    assert FOX_HEADS == DSA_HEADS and FOX_W == DSA_W
    blk = lambda w, col: pl.BlockSpec((None, tq, w), lambda b, i: (b, i, col))
    res = lambda w, col: pl.BlockSpec((None, S, w), lambda b, i: (b, 0, col))
    res_t = lambda h: pl.BlockSpec((None, h, S), lambda b, i: (b, 0, 0))
    return pl.pallas_call(
        functools.partial(_mixers_kernel, topk=topk),
        grid=(B, S // tq),
        in_specs=[blk(FOX_W, 0), res(FOX_W, 1), res_t(FOX_W), res(LANES, 0),
                  blk(IDX_W, 0), res(LANES, IDX_W // LANES),
                  blk(DSA_W, 0), res(LANES, DSA_W // LANES), res_t(HEAD_DIM)],
        out_specs=[blk(FOX_W, 0), blk(DSA_W, 0)],
        out_shape=[jax.ShapeDtypeStruct((B, S, FOX_W), MM_DTYPE),
                   jax.ShapeDtypeStruct((B, S, DSA_W), MM_DTYPE)],
        scratch_shapes=[pltpu.VMEM((S, tq), F32),
                        pltpu.VMEM((2 * SUBLANES, tq), F32)]
                       + _flash_scratch(DSA_HEADS, tq),
        compiler_params=pltpu.CompilerParams(
            dimension_semantics=("arbitrary", "arbitrary"), vmem_limit_bytes=VMEM_LIMIT),
        name="token_mixers",
    )(qk_f, qk_f, vt_f, misc, qk_i, qk_i, qk_d, qk_d, vt_d)


def _merge_kernel(of_ref, od_ref, gate_ref, x_ref, wbf_ref, wbd_ref, wo_ref, g_ref, h_ref):
    a = _dot(of_ref[...], wbf_ref[...])
    b = _dot(od_ref[...], wbd_ref[...])
    mixed = (gate_ref[:, 0:D_MODEL].astype(F32) * a
             + gate_ref[:, D_MODEL:2 * D_MODEL].astype(F32) * b)
    y = _dot(mixed.astype(MM_DTYPE), wo_ref[...])
    h_ref[...] = x_ref[...] + _rms(y, g_ref[...])


def _merge_stage(o_f, o_d, gates, x, wbf, wbd, wo, g_post, *, tm):
    N, D = x.shape
    row = lambda w: pl.BlockSpec((tm, w), lambda t: (t, 0))
    full = lambda a: pl.BlockSpec(a.shape, lambda t: (0,) * a.ndim)
    return pl.pallas_call(
        _merge_kernel,
        grid=(N // tm,),
        in_specs=[row(FOX_W), row(DSA_W), row(2 * D), row(D),
                  full(wbf), full(wbd), full(wo), full(g_post)],
        out_specs=row(D),
        out_shape=jax.ShapeDtypeStruct((N, D), F32),
        compiler_params=pltpu.CompilerParams(
            dimension_semantics=("arbitrary",), vmem_limit_bytes=VMEM_LIMIT),
        name="merge_stage",
    )(o_f, o_d, gates, x, wbf, wbd, wo, g_post)


def _ffn_kernel(h_ref, gpre_ref, wg_ref, wu_ref, wd_ref, gpost_ref, o_ref):
    h = h_ref[...]
    v = _rms(h, gpre_ref[...]).astype(MM_DTYPE)
    g = _dot(v, wg_ref[...])
    a = (g * _sigmoid(g) * _dot(v, wu_ref[...])).astype(MM_DTYPE)
    o_ref[...] = h + _rms(_dot(a, wd_ref[...]), gpost_ref[...])


def _ffn_stage(h, g_pre, wg, wu, wd, g_post, *, tm):
    N, D = h.shape
    full = lambda a: pl.BlockSpec(a.shape, lambda t: (0,) * a.ndim)
    row = pl.BlockSpec((tm, D), lambda t: (t, 0))
    return pl.pallas_call(
        _ffn_kernel,
        grid=(N // tm,),
        in_specs=[row, full(g_pre), full(wg), full(wu), full(wd), full(g_post)],
        out_specs=row,
        out_shape=jax.ShapeDtypeStruct((N, D), F32),
        compiler_params=pltpu.CompilerParams(
            dimension_semantics=("arbitrary",), vmem_limit_bytes=VMEM_LIMIT),
        name="ffn_stage",
    )(h, g_pre, wg, wu, wd, g_post)


def _rope_tables(S, rot, period):
    half = rot // 2
    inv_freq = jnp.float32(ROPE_THETA) ** (-jnp.arange(half, dtype=F32) * 2.0 / rot)
    ang = jnp.arange(S).astype(F32)[:, None] * inv_freq[None, :]
    cos, sin = jnp.cos(ang), jnp.sin(ang)
    one = jnp.ones((S, period - rot), F32)
    zero = jnp.zeros((S, period - rot), F32)
    zh = jnp.zeros((S, half), F32)
    reps = LANES // period
    cos_t = jnp.tile(jnp.concatenate([cos, cos, one], axis=1), (1, reps))
    sin_a = jnp.tile(jnp.concatenate([-sin, zh, zero], axis=1), (1, reps))
    sin_b = jnp.tile(jnp.concatenate([zh, sin, zero], axis=1), (1, reps))
    return cos_t, sin_a, sin_b


def _split_w_in(w):
    parts, off = [], 0
    for width in IN_WIDTHS:
        parts.append(w[:, off:off + width])
        off += width
    return parts


def _layer(h, p, tabs, *, tm_in, tq_mix, tm_merge, tm_ffn):
    B, S, D = h.shape
    (w_qf, w_kf, w_vf, w_fl, w_qd, w_kd, w_vd, w_qi, w_ki, w_wi, w_gf, w_gd) = _split_w_in(p["w_in"])
    cast = lambda a: a.astype(MM_DTYPE)
    pad_m = jnp.zeros((D, LANES - IDX_HEADS - FOX_HEADS), F32)
    wf = cast(jnp.concatenate([w_qf, w_kf, w_vf], axis=1))
    wd = cast(jnp.concatenate([w_qd, w_kd, w_vd], axis=1))
    wi = cast(jnp.concatenate([w_qi, w_ki, w_ki], axis=1))
    wm = cast(jnp.concatenate([w_wi, w_fl, pad_m], axis=1))
    wg = cast(jnp.concatenate([w_gf, w_gd], axis=1))
    bm = jnp.zeros((1, LANES), F32).at[0, MISC_F0:MISC_F0 + FOX_HEADS].set(p["b_forget"].astype(F32))
    bg = p["b_gate"].astype(F32).reshape(1, 2 * D)
    row = lambda a: a.astype(F32).reshape(1, D)

    qk_f, vt_f, qk_d, vt_d, qk_i, misc, gates = _input_stage(
        h, row(p["norm_mix_pre"]), wf, wd, wi, wm, wg, bm, bg, tabs, tm=tm_in)

    o_f, o_d = _token_mixers(qk_f, vt_f, misc, qk_i, qk_d, vt_d, tq=tq_mix,
                             topk=min(TOPK_MAX, S // 4))

    N = B * S
    h1 = _merge_stage(o_f.reshape(N, FOX_W), o_d.reshape(N, DSA_W), gates.reshape(N, 2 * D),
                      h.reshape(N, D), cast(p["w_branch_fox"]), cast(p["w_branch_dsa"]),
                      cast(p["w_out"]), row(p["norm_mix_post"]), tm=tm_merge)
    h2 = _ffn_stage(h1, row(p["norm_ffn_pre"]), cast(p["w_ffn_gate"]), cast(p["w_ffn_up"]),
                    cast(p["w_ffn_down"]), row(p["norm_ffn_post"]), tm=tm_ffn)
    return h2.reshape(B, S, D)


def kernel(x, norm_mix_pre, w_in, b_forget, b_gate, w_branch_fox, w_branch_dsa, w_out,
           norm_mix_post, norm_ffn_pre, w_ffn_gate, w_ffn_up, w_ffn_down, norm_ffn_post):
    B, S, D = x.shape
    params = dict(norm_mix_pre=norm_mix_pre, w_in=w_in, b_forget=b_forget, b_gate=b_gate,
                  w_branch_fox=w_branch_fox, w_branch_dsa=w_branch_dsa, w_out=w_out,
                  norm_mix_post=norm_mix_post, norm_ffn_pre=norm_ffn_pre, w_ffn_gate=w_ffn_gate,
                  w_ffn_up=w_ffn_up, w_ffn_down=w_ffn_down, norm_ffn_post=norm_ffn_post)
    tabs = (_rope_tables(S, HEAD_DIM // ROT_FRAC_DEN, HEAD_DIM)
            + _rope_tables(S, IDX_DIM // ROT_FRAC_DEN, IDX_DIM))
    tiles = dict(tm_in=min(512, S), tq_mix=256,
                 tm_merge=min(1024, B * S), tm_ffn=min(512, B * S))
    h = x
    for l in range(w_in.shape[0]):
        h = _layer(h, {k: v[l] for k, v in params.items()}, tabs, **tiles)
    return h
```

```python
import functools

import jax
import jax.numpy as jnp
from jax import lax
from jax.experimental import pallas as pl
from jax.experimental.pallas import tpu as pltpu

D_MODEL = 1024
HEAD_DIM = 128
FOX_HEADS = 4
DSA_HEADS = 4
IDX_HEADS = 8
IDX_DIM = 64
ROT_FRAC_DEN = 4
ROPE_THETA = 500000.0
TOPK_MAX = 256
D_FF = 2816
RMS_EPS = 1e-6
FOX_W = FOX_HEADS * HEAD_DIM
DSA_W = DSA_HEADS * HEAD_DIM
IDX_W = IDX_HEADS * IDX_DIM
IN_WIDTHS = (FOX_W, FOX_W, FOX_W, FOX_HEADS, DSA_W, HEAD_DIM, HEAD_DIM,
             IDX_W, IDX_DIM, IDX_HEADS, D_MODEL, D_MODEL)

LANES = 128
SUBLANES = 8
VMEM_LIMIT = 56 * 1024 * 1024
MM_DTYPE = jnp.bfloat16
NEG_BIG = -1e30
LOG2E = 1.4426950408889634
Q_SCALE = HEAD_DIM ** -0.5 * LOG2E

F32 = jnp.float32


def _nt_dot(a, b):
    return lax.dot_general(a, b, (((1,), (1,)), ((), ())), preferred_element_type=F32)


def _dot(a, b):
    return jnp.dot(a, b, preferred_element_type=F32)


def _rms(x, g):
    return x * lax.rsqrt(jnp.mean(x * x, axis=-1, keepdims=True) + RMS_EPS) * g


def _sigmoid(z):
    return 1.0 / (1.0 + jnp.exp(-z))


def _sublane_fold(x, op):
    acc = x[0:SUBLANES, :]
    for j in range(1, x.shape[0] // SUBLANES):
        acc = op(acc, x[j * SUBLANES:(j + 1) * SUBLANES, :])
    return acc


MISC_F0 = IDX_HEADS


def _rope(seg, cos, sin_a, sin_b, half):
    return (seg * cos
            + pltpu.roll(seg, LANES - half, axis=1) * sin_a
            + pltpu.roll(seg, half, axis=1) * sin_b)


def _input_kernel(x_ref, g_ref, wf_ref, wd_ref, wi_ref, wm_ref, wg_ref, bm_ref, bg_ref,
                  cd_ref, sad_ref, sbd_ref, ci_ref, sai_ref, sbi_ref,
                  f_ref, vtf_ref, d_ref, vtd_ref, i_ref, m_ref, gate_ref, carry_ref, *, idx_scale):
    tm = x_ref.shape[0]
    u = _rms(x_ref[...], g_ref[...]).astype(MM_DTYPE)

    yf = _dot(u, wf_ref[...])
    f_ref[:, 0:FOX_W] = (yf[:, 0:FOX_W] * Q_SCALE).astype(f_ref.dtype)
    f_ref[:, FOX_W:2 * FOX_W] = yf[:, FOX_W:2 * FOX_W].astype(f_ref.dtype)
    for h in range(FOX_HEADS):
        v = yf[:, 2 * FOX_W + h * HEAD_DIM:2 * FOX_W + (h + 1) * HEAD_DIM]
        vtf_ref[h * HEAD_DIM:(h + 1) * HEAD_DIM, :] = v.T.astype(vtf_ref.dtype)

    yd = _dot(u, wd_ref[...])
    cd, sad, sbd = cd_ref[...], sad_ref[...], sbd_ref[...]
    for j in range(DSA_HEADS + 1):
        seg = _rope(yd[:, j * LANES:(j + 1) * LANES], cd, sad, sbd, HEAD_DIM // ROT_FRAC_DEN // 2)
        if j < DSA_HEADS:
            seg = seg * Q_SCALE
        d_ref[:, j * LANES:(j + 1) * LANES] = seg.astype(d_ref.dtype)
    vtd_ref[...] = yd[:, (DSA_HEADS + 1) * LANES:].T.astype(vtd_ref.dtype)

    yi = _dot(u, wi_ref[...])
    ci, sai, sbi = ci_ref[...], sai_ref[...], sbi_ref[...]
    for j in range(yi.shape[1] // LANES):
        seg = _rope(yi[:, j * LANES:(j + 1) * LANES], ci, sai, sbi, IDX_DIM // ROT_FRAC_DEN // 2)
        i_ref[:, j * LANES:(j + 1) * LANES] = seg.astype(i_ref.dtype)

    gate_ref[...] = _sigmoid(_dot(u, wg_ref[...]) + bg_ref[...]).astype(gate_ref.dtype)

    ym = _dot(u, wm_ref[...])
    col = lax.broadcasted_iota(jnp.int32, ym.shape, 1)
    row = lax.broadcasted_iota(jnp.int32, ym.shape, 0)
    is_f = (col >= MISC_F0) & (col < MISC_F0 + FOX_HEADS)
    z = ym + bm_ref[...]
    log_f = jnp.where(is_f, jnp.minimum(z, 0.0) - jnp.log1p(jnp.exp(-jnp.abs(z))), 0.0)

    @pl.when(pl.program_id(1) == 0)
    def _():
        carry_ref[...] = jnp.zeros_like(carry_ref)

    c = log_f
    shift = 1
    while shift < tm:
        c = c + jnp.where(row >= shift, pltpu.roll(c, shift, axis=0), 0.0)
        shift *= 2
    c = c + carry_ref[0:1, :]
    carry_ref[0:1, :] = c[tm - 1:tm, :]
    m_ref[...] = jnp.where(is_f, c * LOG2E, ym * idx_scale)


def _input_stage(x, gain, wf, wd, wi, wm, wg, bm, bg, tabs, *, tm):
    B, S, D = x.shape
    idx_scale = (IDX_DIM ** -0.5) * (IDX_HEADS ** -0.5)
    full = lambda a: pl.BlockSpec(a.shape, lambda b, t: (0,) * a.ndim,
                                  pipeline_mode=pl.Buffered(1))
    tab = pl.BlockSpec((tm, LANES), lambda b, t: (t, 0))
    rows = lambda w: pl.BlockSpec((None, tm, w), lambda b, t: (b, t, 0))
    cols = lambda h: pl.BlockSpec((None, h, tm), lambda b, t: (b, 0, t))
    qk_d_w = DSA_W + HEAD_DIM
    return pl.pallas_call(
        functools.partial(_input_kernel, idx_scale=idx_scale),
        grid=(B, S // tm),
        in_specs=[rows(D), full(gain), full(wf), full(wd), full(wi), full(wm), full(wg),
                  full(bm), full(bg)] + [tab] * 6,
        out_specs=[rows(2 * FOX_W), cols(FOX_W), rows(qk_d_w), cols(HEAD_DIM),
                   rows(wi.shape[1]), rows(wm.shape[1]), rows(wg.shape[1])],
        out_shape=[jax.ShapeDtypeStruct((B, S, 2 * FOX_W), MM_DTYPE),
                   jax.ShapeDtypeStruct((B, FOX_W, S), MM_DTYPE),
                   jax.ShapeDtypeStruct((B, S, qk_d_w), MM_DTYPE),
                   jax.ShapeDtypeStruct((B, HEAD_DIM, S), MM_DTYPE),
                   jax.ShapeDtypeStruct((B, S, wi.shape[1]), MM_DTYPE),
                   jax.ShapeDtypeStruct((B, S, wm.shape[1]), F32),
                   jax.ShapeDtypeStruct((B, S, wg.shape[1]), MM_DTYPE)],
        scratch_shapes=[pltpu.VMEM((SUBLANES, LANES), F32)],
        compiler_params=pltpu.CompilerParams(
            dimension_semantics=("arbitrary", "arbitrary"), vmem_limit_bytes=VMEM_LIMIT),
        name="input_stage",
    )(x, gain, wf, wd, wi, wm, wg, bm, bg, *tabs)


def _flash_attend(i, qk_fn, mask_fn, pv_fn, o_ref, scratch, heads, tq, side_fn=None):
    s_buf, p_buf, m_s, l_s, a_s, acc_s = scratch
    m_s[...] = jnp.full(m_s.shape, NEG_BIG, F32)
    l_s[...] = jnp.zeros(l_s.shape, F32)
    a_s[...] = jnp.ones(a_s.shape, F32)
    acc_s[...] = jnp.zeros(acc_s.shape, F32)
    p_buf[1] = jnp.zeros(p_buf.shape[1:], p_buf.dtype)
    s_buf[0] = qk_fn(0)

    def lagged_values(c, slot):
        acc_s[...] = a_s[0:1, :] * acc_s[...] + pv_fn(c, p_buf[slot])

    def stage(c, cur, last):
        lagged_values(jnp.maximum(c - 1, 0), 1 - cur)
        if not last:
            s_buf[1 - cur] = qk_fn(c + 1)
        if side_fn is not None:
            side_fn(c, last)
        s = mask_fn(c, s_buf[cur], last)
        m_old = m_s[0:1, :]
        m_new = jnp.maximum(m_old, jnp.max(s, axis=0, keepdims=True))
        alpha = jnp.exp2(m_old - m_new)
        p = jnp.exp2(s - m_new)
        l_s[0:1, :] = alpha * l_s[0:1, :] + jnp.sum(p, axis=0, keepdims=True)
        p_buf[cur] = p.astype(p_buf.dtype)
        a_s[0:1, :] = alpha
        m_s[0:1, :] = m_new

    def pair(j, _):
        stage(2 * j, 0, False)
        stage(2 * j + 1, 1, False)
        return 0

    lax.fori_loop(0, i // 2, pair, 0)

    @pl.when(i % 2 == 0)
    def _():
        stage(i, 0, True)
        lagged_values(i, 0)

    @pl.when(i % 2 == 1)
    def _():
        stage(i - 1, 0, False)
        stage(i, 1, True)
        lagged_values(i, 1)

    for h in range(heads):
        cols = slice(h * tq, (h + 1) * tq)
        o_t = acc_s[:, cols] / l_s[0:1, cols]
        o_ref[:, h * HEAD_DIM:(h + 1) * HEAD_DIM] = o_t.T.astype(o_ref.dtype)


def _flash_scratch(heads, tq):
    w = heads * tq
    return [pltpu.VMEM((2, tq, w), F32),
            pltpu.VMEM((2, tq, w), MM_DTYPE),
            pltpu.VMEM((SUBLANES, w), F32),
            pltpu.VMEM((SUBLANES, w), F32),
            pltpu.VMEM((SUBLANES, w), F32),
            pltpu.VMEM((HEAD_DIM, w), F32)]


def _fox_fns(q_ref, k_ref, vt_ref, m_ref, tq):
    ch = tq
    H = FOX_HEADS
    causal_diag = (lax.broadcasted_iota(jnp.int32, (ch, H * tq), 0)
                   <= lax.broadcasted_iota(jnp.int32, (ch, H * tq), 1) % tq)

    def qk(c):
        start = pl.multiple_of(c * ch, ch)
        parts = []
        for h in range(H):
            hd = slice(h * HEAD_DIM, (h + 1) * HEAD_DIM)
            s = _nt_dot(k_ref[pl.ds(start, ch), hd], q_ref[:, hd])
            parts.append(s - m_ref[pl.ds(start, ch), MISC_F0 + h:MISC_F0 + h + 1])
        return jnp.concatenate(parts, axis=1)

    def mask(c, s, last):
        return jnp.where(causal_diag, s, NEG_BIG) if last else s

    def pv(c, p):
        start = pl.multiple_of(c * ch, ch)
        return jnp.concatenate(
            [_dot(vt_ref[h * HEAD_DIM:(h + 1) * HEAD_DIM, pl.ds(start, ch)],
                  p[:, h * tq:(h + 1) * tq]) for h in range(H)], axis=1)

    return qk, mask, pv


IDX_BITS = 13
SEARCH_MAX_ITERS = 256
SEARCH_FIXED_ITERS = 16
F32_TINY = 1.1754943508222875e-38


def _mixers_kernel(qf_ref, kf_ref, vtf_ref, m_ref, qi_ref, ki_ref, qd_ref, kd_ref, vtd_ref,
                   of_ref, od_ref, sc_ref, ext_ref, *flash_scratch, topk):
    i = pl.program_id(1)
    tq = qi_ref.shape[0]
    ch = tq
    n_ch = i + 1
    key_off = lax.broadcasted_iota(jnp.int32, (ch, tq), 0)
    q_off = lax.broadcasted_iota(jnp.int32, (ch, tq), 1)
    causal_diag = key_off <= q_off

    half_lane = lax.broadcasted_iota(jnp.int32, (tq, LANES), 1) // IDX_DIM
    q_heads = []
    for h in range(IDX_HEADS):
        pair = qi_ref[:, (h // 2) * LANES:(h // 2 + 1) * LANES]
        q_heads.append(jnp.where(half_lane == h % 2, pair, jnp.zeros_like(pair)))
    w_t = m_ref[pl.ds(pl.multiple_of(i * tq, tq), tq), :].T
    ext_ref[0:SUBLANES, :] = jnp.full((SUBLANES, tq), -jnp.inf, F32)
    ext_ref[SUBLANES:2 * SUBLANES, :] = jnp.full((SUBLANES, tq), jnp.inf, F32)

    def score_chunk(c, diag):
        start = pl.multiple_of(c * ch, ch)
        kc = ki_ref[pl.ds(start, ch), :]
        acc = jnp.zeros((ch, tq), F32)
        for h in range(IDX_HEADS):
            acc = acc + jnp.maximum(_nt_dot(kc, q_heads[h]), 0.0) * w_t[h:h + 1, :]
        lo_src = acc
        if diag:
            lo_src = jnp.where(causal_diag, acc, jnp.inf)
            acc = jnp.where(causal_diag, acc, -jnp.inf)
        sc_ref[pl.ds(start, ch), :] = acc
        ext_ref[0:SUBLANES, :] = jnp.maximum(ext_ref[0:SUBLANES, :],
                                             _sublane_fold(acc, jnp.maximum))
        ext_ref[SUBLANES:2 * SUBLANES, :] = jnp.minimum(ext_ref[SUBLANES:2 * SUBLANES, :],
                                                        _sublane_fold(lo_src, jnp.minimum))

    fox_qk, fox_mask, fox_pv = _fox_fns(qf_ref, kf_ref, vtf_ref, m_ref, tq)
    _flash_attend(i, fox_qk, fox_mask, fox_pv, of_ref, flash_scratch, FOX_HEADS, tq,
                  side_fn=score_chunk)

    row_max = jnp.max(ext_ref[0:SUBLANES, :], axis=0, keepdims=True)
    row_min = jnp.min(ext_ref[SUBLANES:2 * SUBLANES, :], axis=0, keepdims=True)

    def count(pred):
        def hits(c):
            start = pl.multiple_of(c * ch, ch)
            hit = pred(sc_ref[pl.ds(start, ch), :], start)
            return _sublane_fold(jnp.where(hit, 1.0, 0.0), jnp.add)
        zero = jnp.zeros((SUBLANES, tq), F32)
        part = lax.fori_loop(0, n_ch // 2, lambda j, cnt: cnt + hits(2 * j) + hits(2 * j + 1), zero)
        part = part + lax.cond(n_ch % 2 == 1, lambda: hits(n_ch - 1), lambda: zero)
        return jnp.sum(part, axis=0, keepdims=True)

    kf = jnp.float32(topk)
    n_causal = (i * tq + 1 + lax.broadcasted_iota(jnp.int32, (1, tq), 1)).astype(F32)
    few = n_causal < kf

    def active(lo, hi, c_lo):
        mid = 0.5 * lo + 0.5 * hi
        return (c_lo > kf) & (mid > lo) & (mid < hi)

    def any_active(state):
        lo, hi, c_lo, it = state[0], state[1], state[2], state[-1]
        busy = jnp.max(jnp.where(active(lo, hi, c_lo), 1.0, 0.0))
        return (busy > 0.5) & (it < SEARCH_MAX_ITERS)

    log_k = jnp.log(kf - 0.5)

    def narrow(state):
        lo, hi, c_lo, c_hi, w_lo, w_hi, last, it = state
        act = active(lo, hi, c_lo)
        mid = 0.5 * lo + 0.5 * hi
        f_lo = (jnp.log(c_lo) - log_k) * w_lo
        f_hi = (log_k - jnp.log(jnp.maximum(c_hi, 0.25))) * w_hi
        guess = lo + (hi - lo) * (f_lo / (f_lo + f_hi))
        cand = jnp.where((guess > lo) & (guess < hi), guess, mid)
        cand = jnp.where((lo < 0.0) & (hi > 0.0), 0.0, cand)
        cand = jnp.where((lo == 0.0) & (hi > F32_TINY), F32_TINY, cand)
        cb = jnp.broadcast_to(cand, (ch, tq))
        cnt = count(lambda s, _: s >= cb)
        up = act & (cnt >= kf)
        dn = act & (cnt < kf)
        w_lo = jnp.where(up, 1.0, jnp.where(dn, jnp.where(last < 0.0, 0.5 * w_lo, 1.0), w_lo))
        w_hi = jnp.where(dn, 1.0, jnp.where(up, jnp.where(last > 0.0, 0.5 * w_hi, 1.0), w_hi))
        last = jnp.where(up, 1.0, jnp.where(dn, -1.0, last))
        return (jnp.where(up, cand, lo), jnp.where(dn, cand, hi),
                jnp.where(up, cnt, c_lo), jnp.where(dn, cnt, c_hi), w_lo, w_hi, last, it + 1)

    above_max = row_max + (row_max - row_min) + 1.0
    ones = jnp.ones((1, tq), F32)
    start_state = (row_min, above_max, jnp.where(few, kf, n_causal), jnp.zeros((1, tq), F32),
                   ones, ones, jnp.zeros((1, tq), F32), jnp.int32(0))
    state = lax.fori_loop(0, SEARCH_FIXED_ITERS, lambda _, st: narrow(st), start_state)
    lo, _, c_lo, c_hi = lax.while_loop(any_active, narrow, state)[:4]

    t = jnp.where(few, -jnp.inf, lo)
    tb = jnp.broadcast_to(t, (ch, tq))

    need = kf - c_hi

    @pl.when(jnp.max(c_lo) > kf)
    def _():
        def idx_step(n, jt):
            cand = jt | (jnp.int32(1) << (IDX_BITS - 1 - n))
            cb = jnp.broadcast_to(cand, (ch, tq))
            below = count(lambda s, start: (s == tb) & (start + key_off < cb))
            return jnp.where(below <= need, cand, jt)
        jt = lax.fori_loop(0, IDX_BITS, idx_step, jnp.zeros((1, tq), jnp.int32))
        jt = jnp.where(c_lo > kf, jt, 2 ** IDX_BITS - 1)
        jb = jnp.broadcast_to(jt, (ch, tq))

        def drop_surplus(c, _):
            start = pl.multiple_of(c * ch, ch)
            s = sc_ref[pl.ds(start, ch), :]
            surplus = (s == tb) & (start + key_off >= jb)
            sc_ref[pl.ds(start, ch), :] = jnp.where(surplus, -jnp.inf, s)
            return 0
        lax.fori_loop(0, n_ch, drop_surplus, 0)

    q_all = jnp.concatenate([qd_ref[:, h * HEAD_DIM:(h + 1) * HEAD_DIM]
                             for h in range(DSA_HEADS)], axis=0)

    def qk(c):
        return _nt_dot(kd_ref[pl.ds(pl.multiple_of(c * ch, ch), ch), :], q_all)

    def mask(c, s, last):
        start = pl.multiple_of(c * ch, ch)
        sel = sc_ref[pl.ds(start, ch), :] >= tb
        if last:
            sel = sel & causal_diag
        return jnp.concatenate([jnp.where(sel, s[:, h * tq:(h + 1) * tq], NEG_BIG)
                                for h in range(DSA_HEADS)], axis=1)

    def pv(c, p):
        return _dot(vtd_ref[:, pl.ds(pl.multiple_of(c * ch, ch), ch)], p)

    _flash_attend(i, qk, mask, pv, od_ref, flash_scratch, DSA_HEADS, tq)


def _token_mixers(qk_f, vt_f, misc, qk_i, qk_d, vt_d, *, tq, topk):
    B, S, _ = qk_d.shape
    assert FOX_HEADS == DSA_HEADS and FOX_W == DSA_W
    blk = lambda w, col: pl.BlockSpec((None, tq, w), lambda b, i: (b, i, col))
    res = lambda w, col: pl.BlockSpec((None, S, w), lambda b, i: (b, 0, col))
    res_t = lambda h: pl.BlockSpec((None, h, S), lambda b, i: (b, 0, 0))
    return pl.pallas_call(
        functools.partial(_mixers_kernel, topk=topk),
        grid=(B, S // tq),
        in_specs=[blk(FOX_W, 0), res(FOX_W, 1), res_t(FOX_W), res(LANES, 0),
                  blk(IDX_W, 0), res(LANES, IDX_W // LANES),
                  blk(DSA_W, 0), res(LANES, DSA_W // LANES), res_t(HEAD_DIM)],
        out_specs=[blk(FOX_W, 0), blk(DSA_W, 0)],
        out_shape=[jax.ShapeDtypeStruct((B, S, FOX_W), MM_DTYPE),
                   jax.ShapeDtypeStruct((B, S, DSA_W), MM_DTYPE)],
        scratch_shapes=[pltpu.VMEM((S, tq), F32),
                        pltpu.VMEM((2 * SUBLANES, tq), F32)]
                       + _flash_scratch(DSA_HEADS, tq),
        compiler_params=pltpu.CompilerParams(
            dimension_semantics=("arbitrary", "arbitrary"), vmem_limit_bytes=VMEM_LIMIT),
        name="token_mixers",
    )(qk_f, qk_f, vt_f, misc, qk_i, qk_i, qk_d, qk_d, vt_d)


def _merge_kernel(of_ref, od_ref, gate_ref, x_ref, wbf_ref, wbd_ref, wo_ref, g_ref, h_ref):
    a = _dot(of_ref[...], wbf_ref[...])
    b = _dot(od_ref[...], wbd_ref[...])
    mixed = (gate_ref[:, 0:D_MODEL].astype(F32) * a
             + gate_ref[:, D_MODEL:2 * D_MODEL].astype(F32) * b)
    y = _dot(mixed.astype(MM_DTYPE), wo_ref[...])
    h_ref[...] = x_ref[...] + _rms(y, g_ref[...])


def _merge_stage(o_f, o_d, gates, x, wbf, wbd, wo, g_post, *, tm):
    N, D = x.shape
    row = lambda w: pl.BlockSpec((tm, w), lambda t: (t, 0))
    full = lambda a: pl.BlockSpec(a.shape, lambda t: (0,) * a.ndim)
    return pl.pallas_call(
        _merge_kernel,
        grid=(N // tm,),
        in_specs=[row(FOX_W), row(DSA_W), row(2 * D), row(D),
                  full(wbf), full(wbd), full(wo), full(g_post)],
        out_specs=row(D),
        out_shape=jax.ShapeDtypeStruct((N, D), F32),
        compiler_params=pltpu.CompilerParams(
            dimension_semantics=("arbitrary",), vmem_limit_bytes=VMEM_LIMIT),
        name="merge_stage",
    )(o_f, o_d, gates, x, wbf, wbd, wo, g_post)


def _ffn_kernel(h_ref, gpre_ref, wg_ref, wu_ref, wd_ref, gpost_ref, o_ref):
    h = h_ref[...]
    v = _rms(h, gpre_ref[...]).astype(MM_DTYPE)
    g = _dot(v, wg_ref[...])
    a = (g * _sigmoid(g) * _dot(v, wu_ref[...])).astype(MM_DTYPE)
    o_ref[...] = h + _rms(_dot(a, wd_ref[...]), gpost_ref[...])


def _ffn_stage(h, g_pre, wg, wu, wd, g_post, *, tm):
    N, D = h.shape
    full = lambda a: pl.BlockSpec(a.shape, lambda t: (0,) * a.ndim)
    row = pl.BlockSpec((tm, D), lambda t: (t, 0))
    return pl.pallas_call(
        _ffn_kernel,
        grid=(N // tm,),
        in_specs=[row, full(g_pre), full(wg), full(wu), full(wd), full(g_post)],
        out_specs=row,
        out_shape=jax.ShapeDtypeStruct((N, D), F32),
        compiler_params=pltpu.CompilerParams(
            dimension_semantics=("arbitrary",), vmem_limit_bytes=VMEM_LIMIT),
        name="ffn_stage",
    )(h, g_pre, wg, wu, wd, g_post)


def _rope_tables(S, rot, period):
    half = rot // 2
    inv_freq = jnp.float32(ROPE_THETA) ** (-jnp.arange(half, dtype=F32) * 2.0 / rot)
    ang = jnp.arange(S).astype(F32)[:, None] * inv_freq[None, :]
    cos, sin = jnp.cos(ang), jnp.sin(ang)
    one = jnp.ones((S, period - rot), F32)
    zero = jnp.zeros((S, period - rot), F32)
    zh = jnp.zeros((S, half), F32)
    reps = LANES // period
    cos_t = jnp.tile(jnp.concatenate([cos, cos, one], axis=1), (1, reps))
    sin_a = jnp.tile(jnp.concatenate([-sin, zh, zero], axis=1), (1, reps))
    sin_b = jnp.tile(jnp.concatenate([zh, sin, zero], axis=1), (1, reps))
    return cos_t, sin_a, sin_b


def _split_w_in(w):
    parts, off = [], 0
    for width in IN_WIDTHS:
        parts.append(w[:, off:off + width])
        off += width
    return parts


def _layer(h, p, tabs, *, tm_in, tq_mix, tm_merge, tm_ffn):
    B, S, D = h.shape
    (w_qf, w_kf, w_vf, w_fl, w_qd, w_kd, w_vd, w_qi, w_ki, w_wi, w_gf, w_gd) = _split_w_in(p["w_in"])
    cast = lambda a: a.astype(MM_DTYPE)
    pad_m = jnp.zeros((D, LANES - IDX_HEADS - FOX_HEADS), F32)
    wf = cast(jnp.concatenate([w_qf, w_kf, w_vf], axis=1))
    wd = cast(jnp.concatenate([w_qd, w_kd, w_vd], axis=1))
    wi = cast(jnp.concatenate([w_qi, w_ki, w_ki], axis=1))
    wm = cast(jnp.concatenate([w_wi, w_fl, pad_m], axis=1))
    wg = cast(jnp.concatenate([w_gf, w_gd], axis=1))
    bm = jnp.zeros((1, LANES), F32).at[0, MISC_F0:MISC_F0 + FOX_HEADS].set(p["b_forget"].astype(F32))
    bg = p["b_gate"].astype(F32).reshape(1, 2 * D)
    row = lambda a: a.astype(F32).reshape(1, D)

    qk_f, vt_f, qk_d, vt_d, qk_i, misc, gates = _input_stage(
        h, row(p["norm_mix_pre"]), wf, wd, wi, wm, wg, bm, bg, tabs, tm=tm_in)

    o_f, o_d = _token_mixers(qk_f, vt_f, misc, qk_i, qk_d, vt_d, tq=tq_mix,
                             topk=min(TOPK_MAX, S // 4))

    N = B * S
    h1 = _merge_stage(o_f.reshape(N, FOX_W), o_d.reshape(N, DSA_W), gates.reshape(N, 2 * D),
                      h.reshape(N, D), cast(p["w_branch_fox"]), cast(p["w_branch_dsa"]),
                      cast(p["w_out"]), row(p["norm_mix_post"]), tm=tm_merge)
    h2 = _ffn_stage(h1, row(p["norm_ffn_pre"]), cast(p["w_ffn_gate"]), cast(p["w_ffn_up"]),
                    cast(p["w_ffn_down"]), row(p["norm_ffn_post"]), tm=tm_ffn)
    return h2.reshape(B, S, D)


def kernel(x, norm_mix_pre, w_in, b_forget, b_gate, w_branch_fox, w_branch_dsa, w_out,
           norm_mix_post, norm_ffn_pre, w_ffn_gate, w_ffn_up, w_ffn_down, norm_ffn_post):
    B, S, D = x.shape
    params = dict(norm_mix_pre=norm_mix_pre, w_in=w_in, b_forget=b_forget, b_gate=b_gate,
                  w_branch_fox=w_branch_fox, w_branch_dsa=w_branch_dsa, w_out=w_out,
                  norm_mix_post=norm_mix_post, norm_ffn_pre=norm_ffn_pre, w_ffn_gate=w_ffn_gate,
                  w_ffn_up=w_ffn_up, w_ffn_down=w_ffn_down, norm_ffn_post=norm_ffn_post)
    tabs = (_rope_tables(S, HEAD_DIM // ROT_FRAC_DEN, HEAD_DIM)
            + _rope_tables(S, IDX_DIM // ROT_FRAC_DEN, IDX_DIM))
    tiles = dict(tm_in=min(1024, S), tq_mix=256,
                 tm_merge=min(1024, B * S), tm_ffn=min(512, B * S))
    h = x
    for l in range(w_in.shape[0]):
        h = _layer(h, {k: v[l] for k, v in params.items()}, tabs, **tiles)
    return h
```

```python
import functools

import jax
import jax.numpy as jnp
from jax import lax
from jax.experimental import pallas as pl
from jax.experimental.pallas import tpu as pltpu

D_MODEL = 1024
HEAD_DIM = 128
FOX_HEADS = 4
DSA_HEADS = 4
IDX_HEADS = 8
IDX_DIM = 64
ROT_FRAC_DEN = 4
ROPE_THETA = 500000.0
TOPK_MAX = 256
D_FF = 2816
RMS_EPS = 1e-6
FOX_W = FOX_HEADS * HEAD_DIM
DSA_W = DSA_HEADS * HEAD_DIM
IDX_W = IDX_HEADS * IDX_DIM
IN_WIDTHS = (FOX_W, FOX_W, FOX_W, FOX_HEADS, DSA_W, HEAD_DIM, HEAD_DIM,
             IDX_W, IDX_DIM, IDX_HEADS, D_MODEL, D_MODEL)

LANES = 128
SUBLANES = 8
VMEM_LIMIT = 56 * 1024 * 1024
MM_DTYPE = jnp.bfloat16
NEG_BIG = -1e30
LOG2E = 1.4426950408889634
Q_SCALE = HEAD_DIM ** -0.5 * LOG2E

F32 = jnp.float32


def _nt_dot(a, b):
    return lax.dot_general(a, b, (((1,), (1,)), ((), ())), preferred_element_type=F32)


def _dot(a, b):
    return jnp.dot(a, b, preferred_element_type=F32)


def _rms(x, g):
    return x * lax.rsqrt(jnp.mean(x * x, axis=-1, keepdims=True) + RMS_EPS) * g


def _sigmoid(z):
    return 1.0 / (1.0 + jnp.exp(-z))


def _sublane_fold(x, op):
    acc = x[0:SUBLANES, :]
    for j in range(1, x.shape[0] // SUBLANES):
        acc = op(acc, x[j * SUBLANES:(j + 1) * SUBLANES, :])
    return acc


MISC_F0 = IDX_HEADS


def _rope(seg, cos, sin_a, sin_b, half):
    return (seg * cos
            + pltpu.roll(seg, LANES - half, axis=1) * sin_a
            + pltpu.roll(seg, half, axis=1) * sin_b)


def _input_kernel(x_ref, g_ref, wf_ref, wd_ref, wi_ref, wm_ref, wg_ref, bm_ref, bg_ref,
                  cd_ref, sad_ref, sbd_ref, ci_ref, sai_ref, sbi_ref,
                  f_ref, vtf_ref, d_ref, vtd_ref, i_ref, m_ref, gate_ref, carry_ref, *, idx_scale):
    tm = x_ref.shape[0]
    u = _rms(x_ref[...], g_ref[...]).astype(MM_DTYPE)

    yf = _dot(u, wf_ref[...])
    f_ref[:, 0:FOX_W] = (yf[:, 0:FOX_W] * Q_SCALE).astype(f_ref.dtype)
    f_ref[:, FOX_W:2 * FOX_W] = yf[:, FOX_W:2 * FOX_W].astype(f_ref.dtype)
    for h in range(FOX_HEADS):
        v = yf[:, 2 * FOX_W + h * HEAD_DIM:2 * FOX_W + (h + 1) * HEAD_DIM]
        vtf_ref[h * HEAD_DIM:(h + 1) * HEAD_DIM, :] = v.T.astype(vtf_ref.dtype)

    yd = _dot(u, wd_ref[...])
    cd, sad, sbd = cd_ref[...], sad_ref[...], sbd_ref[...]
    for j in range(DSA_HEADS + 1):
        seg = _rope(yd[:, j * LANES:(j + 1) * LANES], cd, sad, sbd, HEAD_DIM // ROT_FRAC_DEN // 2)
        if j < DSA_HEADS:
            seg = seg * Q_SCALE
        d_ref[:, j * LANES:(j + 1) * LANES] = seg.astype(d_ref.dtype)
    vtd_ref[...] = yd[:, (DSA_HEADS + 1) * LANES:].T.astype(vtd_ref.dtype)

    yi = _dot(u, wi_ref[...])
    ci, sai, sbi = ci_ref[...], sai_ref[...], sbi_ref[...]
    for j in range(yi.shape[1] // LANES):
        seg = _rope(yi[:, j * LANES:(j + 1) * LANES], ci, sai, sbi, IDX_DIM // ROT_FRAC_DEN // 2)
        i_ref[:, j * LANES:(j + 1) * LANES] = seg.astype(i_ref.dtype)

    gate_ref[...] = _sigmoid(_dot(u, wg_ref[...]) + bg_ref[...]).astype(gate_ref.dtype)

    ym = _dot(u, wm_ref[...])
    col = lax.broadcasted_iota(jnp.int32, ym.shape, 1)
    row = lax.broadcasted_iota(jnp.int32, ym.shape, 0)
    is_f = (col >= MISC_F0) & (col < MISC_F0 + FOX_HEADS)
    z = ym + bm_ref[...]
    log_f = jnp.where(is_f, jnp.minimum(z, 0.0) - jnp.log1p(jnp.exp(-jnp.abs(z))), 0.0)

    @pl.when(pl.program_id(1) == 0)
    def _():
        carry_ref[...] = jnp.zeros_like(carry_ref)

    c = log_f
    shift = 1
    while shift < tm:
        c = c + jnp.where(row >= shift, pltpu.roll(c, shift, axis=0), 0.0)
        shift *= 2
    c = c + carry_ref[0:1, :]
    carry_ref[0:1, :] = c[tm - 1:tm, :]
    m_ref[...] = jnp.where(is_f, c * LOG2E, ym * idx_scale)


def _input_stage(x, gain, wf, wd, wi, wm, wg, bm, bg, tabs, *, tm):
    B, S, D = x.shape
    idx_scale = (IDX_DIM ** -0.5) * (IDX_HEADS ** -0.5)
    full = lambda a: pl.BlockSpec(a.shape, lambda b, t: (0,) * a.ndim,
                                  pipeline_mode=pl.Buffered(1))
    tab = pl.BlockSpec((tm, LANES), lambda b, t: (t, 0))
    rows = lambda w: pl.BlockSpec((None, tm, w), lambda b, t: (b, t, 0))
    cols = lambda h: pl.BlockSpec((None, h, tm), lambda b, t: (b, 0, t))
    qk_d_w = DSA_W + HEAD_DIM
    return pl.pallas_call(
        functools.partial(_input_kernel, idx_scale=idx_scale),
        grid=(B, S // tm),
        in_specs=[rows(D), full(gain), full(wf), full(wd), full(wi), full(wm), full(wg),
                  full(bm), full(bg)] + [tab] * 6,
        out_specs=[rows(2 * FOX_W), cols(FOX_W), rows(qk_d_w), cols(HEAD_DIM),
                   rows(wi.shape[1]), rows(wm.shape[1]), rows(wg.shape[1])],
        out_shape=[jax.ShapeDtypeStruct((B, S, 2 * FOX_W), MM_DTYPE),
                   jax.ShapeDtypeStruct((B, FOX_W, S), MM_DTYPE),
                   jax.ShapeDtypeStruct((B, S, qk_d_w), MM_DTYPE),
                   jax.ShapeDtypeStruct((B, HEAD_DIM, S), MM_DTYPE),
                   jax.ShapeDtypeStruct((B, S, wi.shape[1]), MM_DTYPE),
                   jax.ShapeDtypeStruct((B, S, wm.shape[1]), F32),
                   jax.ShapeDtypeStruct((B, S, wg.shape[1]), MM_DTYPE)],
        scratch_shapes=[pltpu.VMEM((SUBLANES, LANES), F32)],
        compiler_params=pltpu.CompilerParams(
            dimension_semantics=("arbitrary", "arbitrary"), vmem_limit_bytes=VMEM_LIMIT),
        name="input_stage",
    )(x, gain, wf, wd, wi, wm, wg, bm, bg, *tabs)


def _flash_attend(i, qk_fn, mask_fn, pv_fn, o_ref, scratch, heads, tq, side_fn=None):
    s_buf, p_buf, m_s, l_s, a_s, acc_s = scratch
    m_s[...] = jnp.full(m_s.shape, NEG_BIG, F32)
    l_s[...] = jnp.zeros(l_s.shape, F32)
    a_s[...] = jnp.ones(a_s.shape, F32)
    acc_s[...] = jnp.zeros(acc_s.shape, F32)
    p_buf[1] = jnp.zeros(p_buf.shape[1:], p_buf.dtype)
    s_buf[0] = qk_fn(0)

    def lagged_values(c, slot):
        acc_s[...] = a_s[0:1, :] * acc_s[...] + pv_fn(c, p_buf[slot])

    def stage(c, cur, last):
        lagged_values(jnp.maximum(c - 1, 0), 1 - cur)
        if not last:
            s_buf[1 - cur] = qk_fn(c + 1)
        if side_fn is not None:
            side_fn(c, last)
        s = mask_fn(c, s_buf[cur], last)
        m_old = m_s[0:1, :]
        m_new = jnp.maximum(m_old, jnp.max(s, axis=0, keepdims=True))
        alpha = jnp.exp2(m_old - m_new)
        p = jnp.exp2(s - m_new)
        l_s[0:1, :] = alpha * l_s[0:1, :] + jnp.sum(p, axis=0, keepdims=True)
        p_buf[cur] = p.astype(p_buf.dtype)
        a_s[0:1, :] = alpha
        m_s[0:1, :] = m_new

    def group(j, _):
        for r in range(STAGES_PER_TRIP):
            stage(STAGES_PER_TRIP * j + r, r % 2, False)
        return 0

    lax.fori_loop(0, i // STAGES_PER_TRIP, group, 0)
    done = (i // STAGES_PER_TRIP) * STAGES_PER_TRIP

    for left in range(STAGES_PER_TRIP):
        @pl.when(i - done == left)
        def _(left=left):
            for r in range(left):
                stage(done + r, r % 2, False)
            stage(i, left % 2, True)
            lagged_values(i, left % 2)

    for h in range(heads):
        cols = slice(h * tq, (h + 1) * tq)
        o_t = acc_s[:, cols] / l_s[0:1, cols]
        o_ref[:, h * HEAD_DIM:(h + 1) * HEAD_DIM] = o_t.T.astype(o_ref.dtype)


STAGES_PER_TRIP = 4


def _flash_scratch(heads, tq):
    w = heads * tq
    return [pltpu.VMEM((2, tq, w), F32),
            pltpu.VMEM((2, tq, w), MM_DTYPE),
            pltpu.VMEM((SUBLANES, w), F32),
            pltpu.VMEM((SUBLANES, w), F32),
            pltpu.VMEM((SUBLANES, w), F32),
            pltpu.VMEM((HEAD_DIM, w), F32)]


def _fox_fns(q_ref, k_ref, vt_ref, m_ref, tq):
    ch = tq
    H = FOX_HEADS
    causal_diag = (lax.broadcasted_iota(jnp.int32, (ch, H * tq), 0)
                   <= lax.broadcasted_iota(jnp.int32, (ch, H * tq), 1) % tq)

    def qk(c):
        start = pl.multiple_of(c * ch, ch)
        parts = []
        for h in range(H):
            hd = slice(h * HEAD_DIM, (h + 1) * HEAD_DIM)
            s = _nt_dot(k_ref[pl.ds(start, ch), hd], q_ref[:, hd])
            parts.append(s - m_ref[pl.ds(start, ch), MISC_F0 + h:MISC_F0 + h + 1])
        return jnp.concatenate(parts, axis=1)

    def mask(c, s, last):
        return jnp.where(causal_diag, s, NEG_BIG) if last else s

    def pv(c, p):
        start = pl.multiple_of(c * ch, ch)
        return jnp.concatenate(
            [_dot(vt_ref[h * HEAD_DIM:(h + 1) * HEAD_DIM, pl.ds(start, ch)],
                  p[:, h * tq:(h + 1) * tq]) for h in range(H)], axis=1)

    return qk, mask, pv


IDX_BITS = 13
SEARCH_MAX_ITERS = 256
SEARCH_FIXED_ITERS = 16
F32_TINY = 1.1754943508222875e-38


def _mixers_kernel(qf_ref, kf_ref, vtf_ref, m_ref, qi_ref, ki_ref, qd_ref, kd_ref, vtd_ref,
                   of_ref, od_ref, sc_ref, ext_ref, *flash_scratch, topk):
    i = pl.program_id(1)
    tq = qi_ref.shape[0]
    ch = tq
    n_ch = i + 1
    key_off = lax.broadcasted_iota(jnp.int32, (ch, tq), 0)
    q_off = lax.broadcasted_iota(jnp.int32, (ch, tq), 1)
    causal_diag = key_off <= q_off

    half_lane = lax.broadcasted_iota(jnp.int32, (tq, LANES), 1) // IDX_DIM
    q_heads = []
    for h in range(IDX_HEADS):
        pair = qi_ref[:, (h // 2) * LANES:(h // 2 + 1) * LANES]
        q_heads.append(jnp.where(half_lane == h % 2, pair, jnp.zeros_like(pair)))
    w_t = m_ref[pl.ds(pl.multiple_of(i * tq, tq), tq), :].T
    ext_ref[0:SUBLANES, :] = jnp.full((SUBLANES, tq), -jnp.inf, F32)
    ext_ref[SUBLANES:2 * SUBLANES, :] = jnp.full((SUBLANES, tq), jnp.inf, F32)

    def score_chunk(c, diag):
        start = pl.multiple_of(c * ch, ch)
        kc = ki_ref[pl.ds(start, ch), :]
        acc = jnp.zeros((ch, tq), F32)
        for h in range(IDX_HEADS):
            acc = acc + jnp.maximum(_nt_dot(kc, q_heads[h]), 0.0) * w_t[h:h + 1, :]
        lo_src = acc
        if diag:
            lo_src = jnp.where(causal_diag, acc, jnp.inf)
            acc = jnp.where(causal_diag, acc, -jnp.inf)
        sc_ref[pl.ds(start, ch), :] = acc
        ext_ref[0:SUBLANES, :] = jnp.maximum(ext_ref[0:SUBLANES, :],
                                             _sublane_fold(acc, jnp.maximum))
        ext_ref[SUBLANES:2 * SUBLANES, :] = jnp.minimum(ext_ref[SUBLANES:2 * SUBLANES, :],
                                                        _sublane_fold(lo_src, jnp.minimum))

    fox_qk, fox_mask, fox_pv = _fox_fns(qf_ref, kf_ref, vtf_ref, m_ref, tq)
    _flash_attend(i, fox_qk, fox_mask, fox_pv, of_ref, flash_scratch, FOX_HEADS, tq,
                  side_fn=score_chunk)

    row_max = jnp.max(ext_ref[0:SUBLANES, :], axis=0, keepdims=True)
    row_min = jnp.min(ext_ref[SUBLANES:2 * SUBLANES, :], axis=0, keepdims=True)

    def count(pred):
        def hits(c):
            start = pl.multiple_of(c * ch, ch)
            hit = pred(sc_ref[pl.ds(start, ch), :], start)
            return _sublane_fold(jnp.where(hit, 1.0, 0.0), jnp.add)
        zero = jnp.zeros((SUBLANES, tq), F32)
        part = lax.fori_loop(0, n_ch // 2, lambda j, cnt: cnt + hits(2 * j) + hits(2 * j + 1), zero)
        part = part + lax.cond(n_ch % 2 == 1, lambda: hits(n_ch - 1), lambda: zero)
        return jnp.sum(part, axis=0, keepdims=True)

    kf = jnp.float32(topk)
    n_causal = (i * tq + 1 + lax.broadcasted_iota(jnp.int32, (1, tq), 1)).astype(F32)
    few = n_causal < kf

    def active(lo, hi, c_lo):
        mid = 0.5 * lo + 0.5 * hi
        return (c_lo > kf) & (mid > lo) & (mid < hi)

    def any_active(state):
        lo, hi, c_lo, it = state[0], state[1], state[2], state[-1]
        busy = jnp.max(jnp.where(active(lo, hi, c_lo), 1.0, 0.0))
        return (busy > 0.5) & (it < SEARCH_MAX_ITERS)

    log_k = jnp.log(kf - 0.5)

    def narrow(state):
        lo, hi, c_lo, c_hi, w_lo, w_hi, last, it = state
        act = active(lo, hi, c_lo)
        mid = 0.5 * lo + 0.5 * hi
        f_lo = (jnp.log(c_lo) - log_k) * w_lo
        f_hi = (log_k - jnp.log(jnp.maximum(c_hi, 0.25))) * w_hi
        guess = lo + (hi - lo) * (f_lo / (f_lo + f_hi))
        cand = jnp.where((guess > lo) & (guess < hi), guess, mid)
        cand = jnp.where((lo < 0.0) & (hi > 0.0), 0.0, cand)
        cand = jnp.where((lo == 0.0) & (hi > F32_TINY), F32_TINY, cand)
        cb = jnp.broadcast_to(cand, (ch, tq))
        cnt = count(lambda s, _: s >= cb)
        up = act & (cnt >= kf)
        dn = act & (cnt < kf)
        w_lo = jnp.where(up, 1.0, jnp.where(dn, jnp.where(last < 0.0, 0.5 * w_lo, 1.0), w_lo))
        w_hi = jnp.where(dn, 1.0, jnp.where(up, jnp.where(last > 0.0, 0.5 * w_hi, 1.0), w_hi))
        last = jnp.where(up, 1.0, jnp.where(dn, -1.0, last))
        return (jnp.where(up, cand, lo), jnp.where(dn, cand, hi),
                jnp.where(up, cnt, c_lo), jnp.where(dn, cnt, c_hi), w_lo, w_hi, last, it + 1)

    above_max = row_max + (row_max - row_min) + 1.0
    ones = jnp.ones((1, tq), F32)
    start_state = (row_min, above_max, jnp.where(few, kf, n_causal), jnp.zeros((1, tq), F32),
                   ones, ones, jnp.zeros((1, tq), F32), jnp.int32(0))
    state = lax.fori_loop(0, SEARCH_FIXED_ITERS, lambda _, st: narrow(st), start_state)
    lo, _, c_lo, c_hi = lax.while_loop(any_active, narrow, state)[:4]

    t = jnp.where(few, -jnp.inf, lo)
    tb = jnp.broadcast_to(t, (ch, tq))

    need = kf - c_hi

    @pl.when(jnp.max(c_lo) > kf)
    def _():
        def idx_step(n, jt):
            cand = jt | (jnp.int32(1) << (IDX_BITS - 1 - n))
            cb = jnp.broadcast_to(cand, (ch, tq))
            below = count(lambda s, start: (s == tb) & (start + key_off < cb))
            return jnp.where(below <= need, cand, jt)
        jt = lax.fori_loop(0, IDX_BITS, idx_step, jnp.zeros((1, tq), jnp.int32))
        jt = jnp.where(c_lo > kf, jt, 2 ** IDX_BITS - 1)
        jb = jnp.broadcast_to(jt, (ch, tq))

        def drop_surplus(c, _):
            start = pl.multiple_of(c * ch, ch)
            s = sc_ref[pl.ds(start, ch), :]
            surplus = (s == tb) & (start + key_off >= jb)
            sc_ref[pl.ds(start, ch), :] = jnp.where(surplus, -jnp.inf, s)
            return 0
        lax.fori_loop(0, n_ch, drop_surplus, 0)

    q_all = jnp.concatenate([qd_ref[:, h * HEAD_DIM:(h + 1) * HEAD_DIM]
                             for h in range(DSA_HEADS)], axis=0)

    def qk(c):
        return _nt_dot(kd_ref[pl.ds(pl.multiple_of(c * ch, ch), ch), :], q_all)

    def mask(c, s, last):
        start = pl.multiple_of(c * ch, ch)
        sel = sc_ref[pl.ds(start, ch), :] >= tb
        if last:
            sel = sel & causal_diag
        return jnp.concatenate([jnp.where(sel, s[:, h * tq:(h + 1) * tq], NEG_BIG)
                                for h in range(DSA_HEADS)], axis=1)

    def pv(c, p):
        return _dot(vtd_ref[:, pl.ds(pl.multiple_of(c * ch, ch), ch)], p)

    _flash_attend(i, qk, mask, pv, od_ref, flash_scratch, DSA_HEADS, tq)


def _token_mixers(qk_f, vt_f, misc, qk_i, qk_d, vt_d, *, tq, topk):
    B, S, _ = qk_d.shape
    assert FOX_HEADS == DSA_HEADS and FOX_W == DSA_W
    blk = lambda w, col: pl.BlockSpec((None, tq, w), lambda b, i: (b, i, col))
    res = lambda w, col: pl.BlockSpec((None, S, w), lambda b, i: (b, 0, col))
    res_t = lambda h: pl.BlockSpec((None, h, S), lambda b, i: (b, 0, 0))
    return pl.pallas_call(
        functools.partial(_mixers_kernel, topk=topk),
        grid=(B, S // tq),
        in_specs=[blk(FOX_W, 0), res(FOX_W, 1), res_t(FOX_W), res(LANES, 0),
                  blk(IDX_W, 0), res(LANES, IDX_W // LANES),
                  blk(DSA_W, 0), res(LANES, DSA_W // LANES), res_t(HEAD_DIM)],
        out_specs=[blk(FOX_W, 0), blk(DSA_W, 0)],
        out_shape=[jax.ShapeDtypeStruct((B, S, FOX_W), MM_DTYPE),
                   jax.ShapeDtypeStruct((B, S, DSA_W), MM_DTYPE)],
        scratch_shapes=[pltpu.VMEM((S, tq), F32),
                        pltpu.VMEM((2 * SUBLANES, tq), F32)]
                       + _flash_scratch(DSA_HEADS, tq),
        compiler_params=pltpu.CompilerParams(
            dimension_semantics=("arbitrary", "arbitrary"), vmem_limit_bytes=VMEM_LIMIT),
        name="token_mixers",
    )(qk_f, qk_f, vt_f, misc, qk_i, qk_i, qk_d, qk_d, vt_d)


def _merge_kernel(of_ref, od_ref, gate_ref, x_ref, wbf_ref, wbd_ref, wo_ref, g_ref, h_ref):
    a = _dot(of_ref[...], wbf_ref[...])
    b = _dot(od_ref[...], wbd_ref[...])
    mixed = (gate_ref[:, 0:D_MODEL].astype(F32) * a
             + gate_ref[:, D_MODEL:2 * D_MODEL].astype(F32) * b)
    y = _dot(mixed.astype(MM_DTYPE), wo_ref[...])
    h_ref[...] = x_ref[...] + _rms(y, g_ref[...])


def _merge_stage(o_f, o_d, gates, x, wbf, wbd, wo, g_post, *, tm):
    N, D = x.shape
    row = lambda w: pl.BlockSpec((tm, w), lambda t: (t, 0))
    full = lambda a: pl.BlockSpec(a.shape, lambda t: (0,) * a.ndim)
    return pl.pallas_call(
        _merge_kernel,
        grid=(N // tm,),
        in_specs=[row(FOX_W), row(DSA_W), row(2 * D), row(D),
                  full(wbf), full(wbd), full(wo), full(g_post)],
        out_specs=row(D),
        out_shape=jax.ShapeDtypeStruct((N, D), F32),
        compiler_params=pltpu.CompilerParams(
            dimension_semantics=("arbitrary",), vmem_limit_bytes=VMEM_LIMIT),
        name="merge_stage",
    )(o_f, o_d, gates, x, wbf, wbd, wo, g_post)


def _ffn_kernel(h_ref, gpre_ref, wg_ref, wu_ref, wd_ref, gpost_ref, o_ref):
    h = h_ref[...]
    v = _rms(h, gpre_ref[...]).astype(MM_DTYPE)
    g = _dot(v, wg_ref[...])
    a = (g * _sigmoid(g) * _dot(v, wu_ref[...])).astype(MM_DTYPE)
    o_ref[...] = h + _rms(_dot(a, wd_ref[...]), gpost_ref[...])


def _ffn_stage(h, g_pre, wg, wu, wd, g_post, *, tm):
    N, D = h.shape
    full = lambda a: pl.BlockSpec(a.shape, lambda t: (0,) * a.ndim)
    row = pl.BlockSpec((tm, D), lambda t: (t, 0))
    return pl.pallas_call(
        _ffn_kernel,
        grid=(N // tm,),
        in_specs=[row, full(g_pre), full(wg), full(wu), full(wd), full(g_post)],
        out_specs=row,
        out_shape=jax.ShapeDtypeStruct((N, D), F32),
        compiler_params=pltpu.CompilerParams(
            dimension_semantics=("arbitrary",), vmem_limit_bytes=VMEM_LIMIT),
        name="ffn_stage",
    )(h, g_pre, wg, wu, wd, g_post)


def _rope_tables(S, rot, period):
    half = rot // 2
    inv_freq = jnp.float32(ROPE_THETA) ** (-jnp.arange(half, dtype=F32) * 2.0 / rot)
    ang = jnp.arange(S).astype(F32)[:, None] * inv_freq[None, :]
    cos, sin = jnp.cos(ang), jnp.sin(ang)
    one = jnp.ones((S, period - rot), F32)
    zero = jnp.zeros((S, period - rot), F32)
    zh = jnp.zeros((S, half), F32)
    reps = LANES // period
    cos_t = jnp.tile(jnp.concatenate([cos, cos, one], axis=1), (1, reps))
    sin_a = jnp.tile(jnp.concatenate([-sin, zh, zero], axis=1), (1, reps))
    sin_b = jnp.tile(jnp.concatenate([zh, sin, zero], axis=1), (1, reps))
    return cos_t, sin_a, sin_b


def _split_w_in(w):
    parts, off = [], 0
    for width in IN_WIDTHS:
        parts.append(w[:, off:off + width])
        off += width
    return parts


def _layer(h, p, tabs, *, tm_in, tq_mix, tm_merge, tm_ffn):
    B, S, D = h.shape
    (w_qf, w_kf, w_vf, w_fl, w_qd, w_kd, w_vd, w_qi, w_ki, w_wi, w_gf, w_gd) = _split_w_in(p["w_in"])
    cast = lambda a: a.astype(MM_DTYPE)
    pad_m = jnp.zeros((D, LANES - IDX_HEADS - FOX_HEADS), F32)
    wf = cast(jnp.concatenate([w_qf, w_kf, w_vf], axis=1))
    wd = cast(jnp.concatenate([w_qd, w_kd, w_vd], axis=1))
    wi = cast(jnp.concatenate([w_qi, w_ki, w_ki], axis=1))
    wm = cast(jnp.concatenate([w_wi, w_fl, pad_m], axis=1))
    wg = cast(jnp.concatenate([w_gf, w_gd], axis=1))
    bm = jnp.zeros((1, LANES), F32).at[0, MISC_F0:MISC_F0 + FOX_HEADS].set(p["b_forget"].astype(F32))
    bg = p["b_gate"].astype(F32).reshape(1, 2 * D)
    row = lambda a: a.astype(F32).reshape(1, D)

    qk_f, vt_f, qk_d, vt_d, qk_i, misc, gates = _input_stage(
        h, row(p["norm_mix_pre"]), wf, wd, wi, wm, wg, bm, bg, tabs, tm=tm_in)

    o_f, o_d = _token_mixers(qk_f, vt_f, misc, qk_i, qk_d, vt_d, tq=tq_mix,
                             topk=min(TOPK_MAX, S // 4))

    N = B * S
    h1 = _merge_stage(o_f.reshape(N, FOX_W), o_d.reshape(N, DSA_W), gates.reshape(N, 2 * D),
                      h.reshape(N, D), cast(p["w_branch_fox"]), cast(p["w_branch_dsa"]),
                      cast(p["w_out"]), row(p["norm_mix_post"]), tm=tm_merge)
    h2 = _ffn_stage(h1, row(p["norm_ffn_pre"]), cast(p["w_ffn_gate"]), cast(p["w_ffn_up"]),
                    cast(p["w_ffn_down"]), row(p["norm_ffn_post"]), tm=tm_ffn)
    return h2.reshape(B, S, D)


def kernel(x, norm_mix_pre, w_in, b_forget, b_gate, w_branch_fox, w_branch_dsa, w_out,
           norm_mix_post, norm_ffn_pre, w_ffn_gate, w_ffn_up, w_ffn_down, norm_ffn_post):
    B, S, D = x.shape
    params = dict(norm_mix_pre=norm_mix_pre, w_in=w_in, b_forget=b_forget, b_gate=b_gate,
                  w_branch_fox=w_branch_fox, w_branch_dsa=w_branch_dsa, w_out=w_out,
                  norm_mix_post=norm_mix_post, norm_ffn_pre=norm_ffn_pre, w_ffn_gate=w_ffn_gate,
                  w_ffn_up=w_ffn_up, w_ffn_down=w_ffn_down, norm_ffn_post=norm_ffn_post)
    tabs = (_rope_tables(S, HEAD_DIM // ROT_FRAC_DEN, HEAD_DIM)
            + _rope_tables(S, IDX_DIM // ROT_FRAC_DEN, IDX_DIM))
    tiles = dict(tm_in=min(1024, S), tq_mix=256,
                 tm_merge=min(1024, B * S), tm_ffn=min(512, B * S))
    h = x
    for l in range(w_in.shape[0]):
        h = _layer(h, {k: v[l] for k, v in params.items()}, tabs, **tiles)
    return h
```

```python
import functools

import jax
import jax.numpy as jnp
from jax import lax
from jax.experimental import pallas as pl
from jax.experimental.pallas import tpu as pltpu

D_MODEL = 1024
HEAD_DIM = 128
FOX_HEADS = 4
DSA_HEADS = 4
IDX_HEADS = 8
IDX_DIM = 64
ROT_FRAC_DEN = 4
ROPE_THETA = 500000.0
TOPK_MAX = 256
D_FF = 2816
RMS_EPS = 1e-6
FOX_W = FOX_HEADS * HEAD_DIM
DSA_W = DSA_HEADS * HEAD_DIM
IDX_W = IDX_HEADS * IDX_DIM
IN_WIDTHS = (FOX_W, FOX_W, FOX_W, FOX_HEADS, DSA_W, HEAD_DIM, HEAD_DIM,
             IDX_W, IDX_DIM, IDX_HEADS, D_MODEL, D_MODEL)

LANES = 128
SUBLANES = 8
VMEM_LIMIT = 56 * 1024 * 1024
MM_DTYPE = jnp.bfloat16
NEG_BIG = -1e30
LOG2E = 1.4426950408889634
Q_SCALE = HEAD_DIM ** -0.5 * LOG2E

F32 = jnp.float32


def _nt_dot(a, b):
    return lax.dot_general(a, b, (((1,), (1,)), ((), ())), preferred_element_type=F32)


def _dot(a, b):
    return jnp.dot(a, b, preferred_element_type=F32)


def _rms(x, g):
    return x * lax.rsqrt(jnp.mean(x * x, axis=-1, keepdims=True) + RMS_EPS) * g


def _sigmoid(z):
    return 1.0 / (1.0 + jnp.exp(-z))


def _sublane_fold(x, op):
    acc = x[0:SUBLANES, :]
    for j in range(1, x.shape[0] // SUBLANES):
        acc = op(acc, x[j * SUBLANES:(j + 1) * SUBLANES, :])
    return acc


MISC_F0 = IDX_HEADS


def _rope(seg, cos, sin_a, sin_b, half):
    return (seg * cos
            + pltpu.roll(seg, LANES - half, axis=1) * sin_a
            + pltpu.roll(seg, half, axis=1) * sin_b)


def _input_kernel(x_ref, g_ref, wf_ref, wd_ref, wi_ref, wm_ref, wg_ref, bm_ref, bg_ref,
                  cd_ref, sad_ref, sbd_ref, ci_ref, sai_ref, sbi_ref,
                  f_ref, vtf_ref, d_ref, vtd_ref, i_ref, m_ref, gate_ref, carry_ref, *, idx_scale):
    tm = x_ref.shape[0]
    u = _rms(x_ref[...], g_ref[...]).astype(MM_DTYPE)

    yf = _dot(u, wf_ref[...])
    f_ref[:, 0:FOX_W] = (yf[:, 0:FOX_W] * Q_SCALE).astype(f_ref.dtype)
    f_ref[:, FOX_W:2 * FOX_W] = yf[:, FOX_W:2 * FOX_W].astype(f_ref.dtype)
    for h in range(FOX_HEADS):
        v = yf[:, 2 * FOX_W + h * HEAD_DIM:2 * FOX_W + (h + 1) * HEAD_DIM]
        vtf_ref[h * HEAD_DIM:(h + 1) * HEAD_DIM, :] = v.T.astype(vtf_ref.dtype)

    yd = _dot(u, wd_ref[...])
    cd, sad, sbd = cd_ref[...], sad_ref[...], sbd_ref[...]
    for j in range(DSA_HEADS + 1):
        seg = _rope(yd[:, j * LANES:(j + 1) * LANES], cd, sad, sbd, HEAD_DIM // ROT_FRAC_DEN // 2)
        if j < DSA_HEADS:
            seg = seg * Q_SCALE
        d_ref[:, j * LANES:(j + 1) * LANES] = seg.astype(d_ref.dtype)
    vtd_ref[...] = yd[:, (DSA_HEADS + 1) * LANES:].T.astype(vtd_ref.dtype)

    yi = _dot(u, wi_ref[...])
    ci, sai, sbi = ci_ref[...], sai_ref[...], sbi_ref[...]
    for j in range(yi.shape[1] // LANES):
        seg = _rope(yi[:, j * LANES:(j + 1) * LANES], ci, sai, sbi, IDX_DIM // ROT_FRAC_DEN // 2)
        i_ref[:, j * LANES:(j + 1) * LANES] = seg.astype(i_ref.dtype)

    gate_ref[...] = _sigmoid(_dot(u, wg_ref[...]) + bg_ref[...]).astype(gate_ref.dtype)

    ym = _dot(u, wm_ref[...])
    col = lax.broadcasted_iota(jnp.int32, ym.shape, 1)
    row = lax.broadcasted_iota(jnp.int32, ym.shape, 0)
    is_f = (col >= MISC_F0) & (col < MISC_F0 + FOX_HEADS)
    z = ym + bm_ref[...]
    log_f = jnp.where(is_f, jnp.minimum(z, 0.0) - jnp.log1p(jnp.exp(-jnp.abs(z))), 0.0)

    @pl.when(pl.program_id(1) == 0)
    def _():
        carry_ref[...] = jnp.zeros_like(carry_ref)

    c = log_f
    shift = 1
    while shift < tm:
        c = c + jnp.where(row >= shift, pltpu.roll(c, shift, axis=0), 0.0)
        shift *= 2
    c = c + carry_ref[0:1, :]
    carry_ref[0:1, :] = c[tm - 1:tm, :]
    m_ref[...] = jnp.where(is_f, c * LOG2E, ym * idx_scale)


def _input_stage(x, gain, wf, wd, wi, wm, wg, bm, bg, tabs, *, tm):
    B, S, D = x.shape
    idx_scale = (IDX_DIM ** -0.5) * (IDX_HEADS ** -0.5)
    full = lambda a: pl.BlockSpec(a.shape, lambda b, t: (0,) * a.ndim,
                                  pipeline_mode=pl.Buffered(1))
    tab = pl.BlockSpec((tm, LANES), lambda b, t: (t, 0))
    rows = lambda w: pl.BlockSpec((None, tm, w), lambda b, t: (b, t, 0))
    cols = lambda h: pl.BlockSpec((None, h, tm), lambda b, t: (b, 0, t))
    qk_d_w = DSA_W + HEAD_DIM
    return pl.pallas_call(
        functools.partial(_input_kernel, idx_scale=idx_scale),
        grid=(B, S // tm),
        in_specs=[rows(D), full(gain), full(wf), full(wd), full(wi), full(wm), full(wg),
                  full(bm), full(bg)] + [tab] * 6,
        out_specs=[rows(2 * FOX_W), cols(FOX_W), rows(qk_d_w), cols(HEAD_DIM),
                   rows(wi.shape[1]), rows(wm.shape[1]), rows(wg.shape[1])],
        out_shape=[jax.ShapeDtypeStruct((B, S, 2 * FOX_W), MM_DTYPE),
                   jax.ShapeDtypeStruct((B, FOX_W, S), MM_DTYPE),
                   jax.ShapeDtypeStruct((B, S, qk_d_w), MM_DTYPE),
                   jax.ShapeDtypeStruct((B, HEAD_DIM, S), MM_DTYPE),
                   jax.ShapeDtypeStruct((B, S, wi.shape[1]), MM_DTYPE),
                   jax.ShapeDtypeStruct((B, S, wm.shape[1]), F32),
                   jax.ShapeDtypeStruct((B, S, wg.shape[1]), MM_DTYPE)],
        scratch_shapes=[pltpu.VMEM((SUBLANES, LANES), F32)],
        compiler_params=pltpu.CompilerParams(
            dimension_semantics=("arbitrary", "arbitrary"), vmem_limit_bytes=VMEM_LIMIT),
        name="input_stage",
    )(x, gain, wf, wd, wi, wm, wg, bm, bg, *tabs)


def _flash_attend(i, qk_fn, mask_fn, pv_fn, o_ref, scratch, heads, tq, side_fn=None):
    s_buf, p_buf, m_s, l_s, a_s, acc_s = scratch
    m_s[...] = jnp.full(m_s.shape, NEG_BIG, F32)
    l_s[...] = jnp.zeros(l_s.shape, F32)
    a_s[...] = jnp.ones(a_s.shape, F32)
    acc_s[...] = jnp.zeros(acc_s.shape, F32)
    p_buf[1] = jnp.zeros(p_buf.shape[1:], p_buf.dtype)
    s_buf[0] = qk_fn(0)

    def lagged_values(c, slot):
        acc_s[...] = a_s[0:1, :] * acc_s[...] + pv_fn(c, p_buf[slot])

    def stage(c, cur, last):
        lagged_values(jnp.maximum(c - 1, 0), 1 - cur)
        if not last:
            s_buf[1 - cur] = qk_fn(c + 1)
        if side_fn is not None:
            side_fn(c, last)
        s = mask_fn(c, s_buf[cur], last)
        m_old = m_s[0:1, :]
        m_new = jnp.maximum(m_old, jnp.max(s, axis=0, keepdims=True))
        alpha = jnp.exp2(m_old - m_new)
        p = jnp.exp2(s - m_new)
        l_s[0:1, :] = alpha * l_s[0:1, :] + jnp.sum(p, axis=0, keepdims=True)
        p_buf[cur] = p.astype(p_buf.dtype)
        a_s[0:1, :] = alpha
        m_s[0:1, :] = m_new

    def group(j, _):
        for r in range(STAGES_PER_TRIP):
            stage(STAGES_PER_TRIP * j + r, r % 2, False)
        return 0

    lax.fori_loop(0, i // STAGES_PER_TRIP, group, 0)
    done = (i // STAGES_PER_TRIP) * STAGES_PER_TRIP

    for left in range(STAGES_PER_TRIP):
        @pl.when(i - done == left)
        def _(left=left):
            for r in range(left):
                stage(done + r, r % 2, False)
            stage(i, left % 2, True)
            lagged_values(i, left % 2)

    for h in range(heads):
        cols = slice(h * tq, (h + 1) * tq)
        o_t = acc_s[:, cols] / l_s[0:1, cols]
        o_ref[:, h * HEAD_DIM:(h + 1) * HEAD_DIM] = o_t.T.astype(o_ref.dtype)


STAGES_PER_TRIP = 8


def _flash_scratch(heads, tq):
    w = heads * tq
    return [pltpu.VMEM((2, tq, w), F32),
            pltpu.VMEM((2, tq, w), MM_DTYPE),
            pltpu.VMEM((SUBLANES, w), F32),
            pltpu.VMEM((SUBLANES, w), F32),
            pltpu.VMEM((SUBLANES, w), F32),
            pltpu.VMEM((HEAD_DIM, w), F32)]


def _fox_fns(q_ref, k_ref, vt_ref, m_ref, tq):
    ch = tq
    H = FOX_HEADS
    causal_diag = (lax.broadcasted_iota(jnp.int32, (ch, H * tq), 0)
                   <= lax.broadcasted_iota(jnp.int32, (ch, H * tq), 1) % tq)

    def qk(c):
        start = pl.multiple_of(c * ch, ch)
        parts = []
        for h in range(H):
            hd = slice(h * HEAD_DIM, (h + 1) * HEAD_DIM)
            s = _nt_dot(k_ref[pl.ds(start, ch), hd], q_ref[:, hd])
            parts.append(s - m_ref[pl.ds(start, ch), MISC_F0 + h:MISC_F0 + h + 1])
        return jnp.concatenate(parts, axis=1)

    def mask(c, s, last):
        return jnp.where(causal_diag, s, NEG_BIG) if last else s

    def pv(c, p):
        start = pl.multiple_of(c * ch, ch)
        return jnp.concatenate(
            [_dot(vt_ref[h * HEAD_DIM:(h + 1) * HEAD_DIM, pl.ds(start, ch)],
                  p[:, h * tq:(h + 1) * tq]) for h in range(H)], axis=1)

    return qk, mask, pv


IDX_BITS = 13
SEARCH_MAX_ITERS = 256
SEARCH_FIXED_ITERS = 18
F32_TINY = 1.1754943508222875e-38


def _mixers_kernel(qf_ref, kf_ref, vtf_ref, m_ref, qi_ref, ki_ref, qd_ref, kd_ref, vtd_ref,
                   of_ref, od_ref, sc_ref, ext_ref, *flash_scratch, topk):
    i = pl.program_id(1)
    tq = qi_ref.shape[0]
    ch = tq
    n_ch = i + 1
    key_off = lax.broadcasted_iota(jnp.int32, (ch, tq), 0)
    q_off = lax.broadcasted_iota(jnp.int32, (ch, tq), 1)
    causal_diag = key_off <= q_off

    half_lane = lax.broadcasted_iota(jnp.int32, (tq, LANES), 1) // IDX_DIM
    q_heads = []
    for h in range(IDX_HEADS):
        pair = qi_ref[:, (h // 2) * LANES:(h // 2 + 1) * LANES]
        q_heads.append(jnp.where(half_lane == h % 2, pair, jnp.zeros_like(pair)))
    w_t = m_ref[pl.ds(pl.multiple_of(i * tq, tq), tq), :].T
    ext_ref[0:SUBLANES, :] = jnp.full((SUBLANES, tq), -jnp.inf, F32)
    ext_ref[SUBLANES:2 * SUBLANES, :] = jnp.full((SUBLANES, tq), jnp.inf, F32)

    def score_chunk(c, diag):
        start = pl.multiple_of(c * ch, ch)
        kc = ki_ref[pl.ds(start, ch), :]
        acc = jnp.zeros((ch, tq), F32)
        for h in range(IDX_HEADS):
            acc = acc + jnp.maximum(_nt_dot(kc, q_heads[h]), 0.0) * w_t[h:h + 1, :]
        lo_src = acc
        if diag:
            lo_src = jnp.where(causal_diag, acc, jnp.inf)
            acc = jnp.where(causal_diag, acc, -jnp.inf)
        sc_ref[pl.ds(start, ch), :] = acc
        ext_ref[0:SUBLANES, :] = jnp.maximum(ext_ref[0:SUBLANES, :],
                                             _sublane_fold(acc, jnp.maximum))
        ext_ref[SUBLANES:2 * SUBLANES, :] = jnp.minimum(ext_ref[SUBLANES:2 * SUBLANES, :],
                                                        _sublane_fold(lo_src, jnp.minimum))

    fox_qk, fox_mask, fox_pv = _fox_fns(qf_ref, kf_ref, vtf_ref, m_ref, tq)
    _flash_attend(i, fox_qk, fox_mask, fox_pv, of_ref, flash_scratch, FOX_HEADS, tq,
                  side_fn=score_chunk)

    row_max = jnp.max(ext_ref[0:SUBLANES, :], axis=0, keepdims=True)
    row_min = jnp.min(ext_ref[SUBLANES:2 * SUBLANES, :], axis=0, keepdims=True)

    def count(pred):
        def hits(c):
            start = pl.multiple_of(c * ch, ch)
            hit = pred(sc_ref[pl.ds(start, ch), :], start)
            return _sublane_fold(jnp.where(hit, 1.0, 0.0), jnp.add)
        zero = jnp.zeros((SUBLANES, tq), F32)
        part = lax.fori_loop(0, n_ch // 2, lambda j, cnt: cnt + hits(2 * j) + hits(2 * j + 1), zero)
        part = part + lax.cond(n_ch % 2 == 1, lambda: hits(n_ch - 1), lambda: zero)
        return jnp.sum(part, axis=0, keepdims=True)

    kf = jnp.float32(topk)
    n_causal = (i * tq + 1 + lax.broadcasted_iota(jnp.int32, (1, tq), 1)).astype(F32)
    few = n_causal < kf

    def active(lo, hi, c_lo):
        mid = 0.5 * lo + 0.5 * hi
        return (c_lo > kf) & (mid > lo) & (mid < hi)

    def any_active(state):
        lo, hi, c_lo, it = state[0], state[1], state[2], state[-1]
        busy = jnp.max(jnp.where(active(lo, hi, c_lo), 1.0, 0.0))
        return (busy > 0.5) & (it < SEARCH_MAX_ITERS)

    log_k = jnp.log(kf - 0.5)

    def narrow(state):
        lo, hi, c_lo, c_hi, w_lo, w_hi, last, it = state
        act = active(lo, hi, c_lo)
        mid = 0.5 * lo + 0.5 * hi
        f_lo = (jnp.log(c_lo) - log_k) * w_lo
        f_hi = (log_k - jnp.log(jnp.maximum(c_hi, 0.25))) * w_hi
        guess = lo + (hi - lo) * (f_lo / (f_lo + f_hi))
        cand = jnp.where((guess > lo) & (guess < hi), guess, mid)
        cand = jnp.where((lo < 0.0) & (hi > 0.0), 0.0, cand)
        cand = jnp.where((lo == 0.0) & (hi > F32_TINY), F32_TINY, cand)
        cb = jnp.broadcast_to(cand, (ch, tq))
        cnt = count(lambda s, _: s >= cb)
        up = act & (cnt >= kf)
        dn = act & (cnt < kf)
        w_lo = jnp.where(up, 1.0, jnp.where(dn, jnp.where(last < 0.0, 0.5 * w_lo, 1.0), w_lo))
        w_hi = jnp.where(dn, 1.0, jnp.where(up, jnp.where(last > 0.0, 0.5 * w_hi, 1.0), w_hi))
        last = jnp.where(up, 1.0, jnp.where(dn, -1.0, last))
        return (jnp.where(up, cand, lo), jnp.where(dn, cand, hi),
                jnp.where(up, cnt, c_lo), jnp.where(dn, cnt, c_hi), w_lo, w_hi, last, it + 1)

    above_max = row_max + (row_max - row_min) + 1.0
    ones = jnp.ones((1, tq), F32)
    start_state = (row_min, above_max, jnp.where(few, kf, n_causal), jnp.zeros((1, tq), F32),
                   ones, ones, jnp.zeros((1, tq), F32), jnp.int32(0))
    state = lax.fori_loop(0, SEARCH_FIXED_ITERS, lambda _, st: narrow(st), start_state)
    lo, _, c_lo, c_hi = lax.while_loop(any_active, narrow, state)[:4]

    t = jnp.where(few, -jnp.inf, lo)
    tb = jnp.broadcast_to(t, (ch, tq))

    need = kf - c_hi

    @pl.when(jnp.max(c_lo) > kf)
    def _():
        def idx_step(n, jt):
            cand = jt | (jnp.int32(1) << (IDX_BITS - 1 - n))
            cb = jnp.broadcast_to(cand, (ch, tq))
            below = count(lambda s, start: (s == tb) & (start + key_off < cb))
            return jnp.where(below <= need, cand, jt)
        jt = lax.fori_loop(0, IDX_BITS, idx_step, jnp.zeros((1, tq), jnp.int32))
        jt = jnp.where(c_lo > kf, jt, 2 ** IDX_BITS - 1)
        jb = jnp.broadcast_to(jt, (ch, tq))

        def drop_surplus(c, _):
            start = pl.multiple_of(c * ch, ch)
            s = sc_ref[pl.ds(start, ch), :]
            surplus = (s == tb) & (start + key_off >= jb)
            sc_ref[pl.ds(start, ch), :] = jnp.where(surplus, -jnp.inf, s)
            return 0
        lax.fori_loop(0, n_ch, drop_surplus, 0)

    q_all = jnp.concatenate([qd_ref[:, h * HEAD_DIM:(h + 1) * HEAD_DIM]
                             for h in range(DSA_HEADS)], axis=0)

    def qk(c):
        return _nt_dot(kd_ref[pl.ds(pl.multiple_of(c * ch, ch), ch), :], q_all)

    def mask(c, s, last):
        start = pl.multiple_of(c * ch, ch)
        sel = sc_ref[pl.ds(start, ch), :] >= tb
        if last:
            sel = sel & causal_diag
        return jnp.concatenate([jnp.where(sel, s[:, h * tq:(h + 1) * tq], NEG_BIG)
                                for h in range(DSA_HEADS)], axis=1)

    def pv(c, p):
        return _dot(vtd_ref[:, pl.ds(pl.multiple_of(c * ch, ch), ch)], p)

    _flash_attend(i, qk, mask, pv, od_ref, flash_scratch, DSA_HEADS, tq)


def _token_mixers(qk_f, vt_f, misc, qk_i, qk_d, vt_d, *, tq, topk):
    B, S, _ = qk_d.shape
    assert FOX_HEADS == DSA_HEADS and FOX_W == DSA_W
    blk = lambda w, col: pl.BlockSpec((None, tq, w), lambda b, i: (b, i, col))
    res = lambda w, col: pl.BlockSpec((None, S, w), lambda b, i: (b, 0, col))
    res_t = lambda h: pl.BlockSpec((None, h, S), lambda b, i: (b, 0, 0))
    return pl.pallas_call(
        functools.partial(_mixers_kernel, topk=topk),
        grid=(B, S // tq),
        in_specs=[blk(FOX_W, 0), res(FOX_W, 1), res_t(FOX_W), res(LANES, 0),
                  blk(IDX_W, 0), res(LANES, IDX_W // LANES),
                  blk(DSA_W, 0), res(LANES, DSA_W // LANES), res_t(HEAD_DIM)],
        out_specs=[blk(FOX_W, 0), blk(DSA_W, 0)],
        out_shape=[jax.ShapeDtypeStruct((B, S, FOX_W), MM_DTYPE),
                   jax.ShapeDtypeStruct((B, S, DSA_W), MM_DTYPE)],
        scratch_shapes=[pltpu.VMEM((S, tq), F32),
                        pltpu.VMEM((2 * SUBLANES, tq), F32)]
                       + _flash_scratch(DSA_HEADS, tq),
        compiler_params=pltpu.CompilerParams(
            dimension_semantics=("arbitrary", "arbitrary"), vmem_limit_bytes=VMEM_LIMIT),
        name="token_mixers",
    )(qk_f, qk_f, vt_f, misc, qk_i, qk_i, qk_d, qk_d, vt_d)


def _merge_kernel(of_ref, od_ref, gate_ref, x_ref, wbf_ref, wbd_ref, wo_ref, g_ref, h_ref):
    a = _dot(of_ref[...], wbf_ref[...])
    b = _dot(od_ref[...], wbd_ref[...])
    mixed = (gate_ref[:, 0:D_MODEL].astype(F32) * a
             + gate_ref[:, D_MODEL:2 * D_MODEL].astype(F32) * b)
    y = _dot(mixed.astype(MM_DTYPE), wo_ref[...])
    h_ref[...] = x_ref[...] + _rms(y, g_ref[...])


def _merge_stage(o_f, o_d, gates, x, wbf, wbd, wo, g_post, *, tm):
    N, D = x.shape
    row = lambda w: pl.BlockSpec((tm, w), lambda t: (t, 0))
    full = lambda a: pl.BlockSpec(a.shape, lambda t: (0,) * a.ndim)
    return pl.pallas_call(
        _merge_kernel,
        grid=(N // tm,),
        in_specs=[row(FOX_W), row(DSA_W), row(2 * D), row(D),
                  full(wbf), full(wbd), full(wo), full(g_post)],
        out_specs=row(D),
        out_shape=jax.ShapeDtypeStruct((N, D), F32),
        compiler_params=pltpu.CompilerParams(
            dimension_semantics=("arbitrary",), vmem_limit_bytes=VMEM_LIMIT),
        name="merge_stage",
    )(o_f, o_d, gates, x, wbf, wbd, wo, g_post)


def _ffn_kernel(h_ref, gpre_ref, wg_ref, wu_ref, wd_ref, gpost_ref, o_ref):
    h = h_ref[...]
    v = _rms(h, gpre_ref[...]).astype(MM_DTYPE)
    g = _dot(v, wg_ref[...])
    a = (g * _sigmoid(g) * _dot(v, wu_ref[...])).astype(MM_DTYPE)
    o_ref[...] = h + _rms(_dot(a, wd_ref[...]), gpost_ref[...])


def _ffn_stage(h, g_pre, wg, wu, wd, g_post, *, tm):
    N, D = h.shape
    full = lambda a: pl.BlockSpec(a.shape, lambda t: (0,) * a.ndim)
    row = pl.BlockSpec((tm, D), lambda t: (t, 0))
    return pl.pallas_call(
        _ffn_kernel,
        grid=(N // tm,),
        in_specs=[row, full(g_pre), full(wg), full(wu), full(wd), full(g_post)],
        out_specs=row,
        out_shape=jax.ShapeDtypeStruct((N, D), F32),
        compiler_params=pltpu.CompilerParams(
            dimension_semantics=("arbitrary",), vmem_limit_bytes=VMEM_LIMIT),
        name="ffn_stage",
    )(h, g_pre, wg, wu, wd, g_post)


def _rope_tables(S, rot, period):
    half = rot // 2
    inv_freq = jnp.float32(ROPE_THETA) ** (-jnp.arange(half, dtype=F32) * 2.0 / rot)
    ang = jnp.arange(S).astype(F32)[:, None] * inv_freq[None, :]
    cos, sin = jnp.cos(ang), jnp.sin(ang)
    one = jnp.ones((S, period - rot), F32)
    zero = jnp.zeros((S, period - rot), F32)
    zh = jnp.zeros((S, half), F32)
    reps = LANES // period
    cos_t = jnp.tile(jnp.concatenate([cos, cos, one], axis=1), (1, reps))
    sin_a = jnp.tile(jnp.concatenate([-sin, zh, zero], axis=1), (1, reps))
    sin_b = jnp.tile(jnp.concatenate([zh, sin, zero], axis=1), (1, reps))
    return cos_t, sin_a, sin_b


def _split_w_in(w):
    parts, off = [], 0
    for width in IN_WIDTHS:
        parts.append(w[:, off:off + width])
        off += width
    return parts


def _layer(h, p, tabs, *, tm_in, tq_mix, tm_merge, tm_ffn):
    B, S, D = h.shape
    (w_qf, w_kf, w_vf, w_fl, w_qd, w_kd, w_vd, w_qi, w_ki, w_wi, w_gf, w_gd) = _split_w_in(p["w_in"])
    cast = lambda a: a.astype(MM_DTYPE)
    pad_m = jnp.zeros((D, LANES - IDX_HEADS - FOX_HEADS), F32)
    wf = cast(jnp.concatenate([w_qf, w_kf, w_vf], axis=1))
    wd = cast(jnp.concatenate([w_qd, w_kd, w_vd], axis=1))
    wi = cast(jnp.concatenate([w_qi, w_ki, w_ki], axis=1))
    wm = cast(jnp.concatenate([w_wi, w_fl, pad_m], axis=1))
    wg = cast(jnp.concatenate([w_gf, w_gd], axis=1))
    bm = jnp.zeros((1, LANES), F32).at[0, MISC_F0:MISC_F0 + FOX_HEADS].set(p["b_forget"].astype(F32))
    bg = p["b_gate"].astype(F32).reshape(1, 2 * D)
    row = lambda a: a.astype(F32).reshape(1, D)

    qk_f, vt_f, qk_d, vt_d, qk_i, misc, gates = _input_stage(
        h, row(p["norm_mix_pre"]), wf, wd, wi, wm, wg, bm, bg, tabs, tm=tm_in)

    o_f, o_d = _token_mixers(qk_f, vt_f, misc, qk_i, qk_d, vt_d, tq=tq_mix,
                             topk=min(TOPK_MAX, S // 4))

    N = B * S
    h1 = _merge_stage(o_f.reshape(N, FOX_W), o_d.reshape(N, DSA_W), gates.reshape(N, 2 * D),
                      h.reshape(N, D), cast(p["w_branch_fox"]), cast(p["w_branch_dsa"]),
                      cast(p["w_out"]), row(p["norm_mix_post"]), tm=tm_merge)
    h2 = _ffn_stage(h1, row(p["norm_ffn_pre"]), cast(p["w_ffn_gate"]), cast(p["w_ffn_up"]),
                    cast(p["w_ffn_down"]), row(p["norm_ffn_post"]), tm=tm_ffn)
    return h2.reshape(B, S, D)


def kernel(x, norm_mix_pre, w_in, b_forget, b_gate, w_branch_fox, w_branch_dsa, w_out,
           norm_mix_post, norm_ffn_pre, w_ffn_gate, w_ffn_up, w_ffn_down, norm_ffn_post):
    B, S, D = x.shape
    params = dict(norm_mix_pre=norm_mix_pre, w_in=w_in, b_forget=b_forget, b_gate=b_gate,
                  w_branch_fox=w_branch_fox, w_branch_dsa=w_branch_dsa, w_out=w_out,
                  norm_mix_post=norm_mix_post, norm_ffn_pre=norm_ffn_pre, w_ffn_gate=w_ffn_gate,
                  w_ffn_up=w_ffn_up, w_ffn_down=w_ffn_down, norm_ffn_post=norm_ffn_post)
    tabs = (_rope_tables(S, HEAD_DIM // ROT_FRAC_DEN, HEAD_DIM)
            + _rope_tables(S, IDX_DIM // ROT_FRAC_DEN, IDX_DIM))
    tiles = dict(tm_in=min(1024, S), tq_mix=256,
                 tm_merge=min(1024, B * S), tm_ffn=min(512, B * S))
    h = x
    for l in range(w_in.shape[0]):
        h = _layer(h, {k: v[l] for k, v in params.items()}, tabs, **tiles)
    return h
```

```python
import functools

import jax
import jax.numpy as jnp
from jax import lax
from jax.experimental import pallas as pl
from jax.experimental.pallas import tpu as pltpu

D_MODEL = 1024
HEAD_DIM = 128
FOX_HEADS = 4
DSA_HEADS = 4
IDX_HEADS = 8
IDX_DIM = 64
ROT_FRAC_DEN = 4
ROPE_THETA = 500000.0
TOPK_MAX = 256
D_FF = 2816
RMS_EPS = 1e-6
FOX_W = FOX_HEADS * HEAD_DIM
DSA_W = DSA_HEADS * HEAD_DIM
IDX_W = IDX_HEADS * IDX_DIM
IN_WIDTHS = (FOX_W, FOX_W, FOX_W, FOX_HEADS, DSA_W, HEAD_DIM, HEAD_DIM,
             IDX_W, IDX_DIM, IDX_HEADS, D_MODEL, D_MODEL)

LANES = 128
SUBLANES = 8
VMEM_LIMIT = 56 * 1024 * 1024
MM_DTYPE = jnp.bfloat16
NEG_BIG = -1e30
LOG2E = 1.4426950408889634
Q_SCALE = HEAD_DIM ** -0.5 * LOG2E

F32 = jnp.float32


def _nt_dot(a, b):
    return lax.dot_general(a, b, (((1,), (1,)), ((), ())), preferred_element_type=F32)


def _dot(a, b):
    return jnp.dot(a, b, preferred_element_type=F32)


def _rms(x, g):
    return x * lax.rsqrt(jnp.mean(x * x, axis=-1, keepdims=True) + RMS_EPS) * g


def _sigmoid(z):
    return 1.0 / (1.0 + jnp.exp(-z))


def _sublane_fold(x, op):
    acc = x[0:SUBLANES, :]
    for j in range(1, x.shape[0] // SUBLANES):
        acc = op(acc, x[j * SUBLANES:(j + 1) * SUBLANES, :])
    return acc


MISC_F0 = IDX_HEADS


def _rope(seg, cos, sin_a, sin_b, half):
    return (seg * cos
            + pltpu.roll(seg, LANES - half, axis=1) * sin_a
            + pltpu.roll(seg, half, axis=1) * sin_b)


def _input_kernel(x_ref, g_ref, wf_ref, wd_ref, wi_ref, wm_ref, wg_ref, bm_ref, bg_ref,
                  cd_ref, sad_ref, sbd_ref, ci_ref, sai_ref, sbi_ref,
                  f_ref, vtf_ref, d_ref, vtd_ref, i_ref, m_ref, gate_ref, carry_ref, *, idx_scale):
    tm = x_ref.shape[0]
    u = _rms(x_ref[...], g_ref[...]).astype(MM_DTYPE)

    yf = _dot(u, wf_ref[...])
    f_ref[:, 0:FOX_W] = (yf[:, 0:FOX_W] * Q_SCALE).astype(f_ref.dtype)
    f_ref[:, FOX_W:2 * FOX_W] = yf[:, FOX_W:2 * FOX_W].astype(f_ref.dtype)
    for h in range(FOX_HEADS):
        v = yf[:, 2 * FOX_W + h * HEAD_DIM:2 * FOX_W + (h + 1) * HEAD_DIM]
        vtf_ref[h * HEAD_DIM:(h + 1) * HEAD_DIM, :] = v.T.astype(vtf_ref.dtype)

    yd = _dot(u, wd_ref[...])
    cd, sad, sbd = cd_ref[...], sad_ref[...], sbd_ref[...]
    for j in range(DSA_HEADS + 1):
        seg = _rope(yd[:, j * LANES:(j + 1) * LANES], cd, sad, sbd, HEAD_DIM // ROT_FRAC_DEN // 2)
        if j < DSA_HEADS:
            seg = seg * Q_SCALE
        d_ref[:, j * LANES:(j + 1) * LANES] = seg.astype(d_ref.dtype)
    vtd_ref[...] = yd[:, (DSA_HEADS + 1) * LANES:].T.astype(vtd_ref.dtype)

    yi = _dot(u, wi_ref[...])
    ci, sai, sbi = ci_ref[...], sai_ref[...], sbi_ref[...]
    for j in range(yi.shape[1] // LANES):
        seg = _rope(yi[:, j * LANES:(j + 1) * LANES], ci, sai, sbi, IDX_DIM // ROT_FRAC_DEN // 2)
        i_ref[:, j * LANES:(j + 1) * LANES] = seg.astype(i_ref.dtype)

    gate_ref[...] = _sigmoid(_dot(u, wg_ref[...]) + bg_ref[...]).astype(gate_ref.dtype)

    ym = _dot(u, wm_ref[...])
    col = lax.broadcasted_iota(jnp.int32, ym.shape, 1)
    row = lax.broadcasted_iota(jnp.int32, ym.shape, 0)
    is_f = (col >= MISC_F0) & (col < MISC_F0 + FOX_HEADS)
    z = ym + bm_ref[...]
    log_f = jnp.where(is_f, jnp.minimum(z, 0.0) - jnp.log1p(jnp.exp(-jnp.abs(z))), 0.0)

    @pl.when(pl.program_id(1) == 0)
    def _():
        carry_ref[...] = jnp.zeros_like(carry_ref)

    c = log_f
    shift = 1
    while shift < tm:
        c = c + jnp.where(row >= shift, pltpu.roll(c, shift, axis=0), 0.0)
        shift *= 2
    c = c + carry_ref[0:1, :]
    carry_ref[0:1, :] = c[tm - 1:tm, :]
    m_ref[...] = jnp.where(is_f, c * LOG2E, ym * idx_scale)


def _input_stage(x, gain, wf, wd, wi, wm, wg, bm, bg, tabs, *, tm):
    B, S, D = x.shape
    idx_scale = (IDX_DIM ** -0.5) * (IDX_HEADS ** -0.5)
    full = lambda a: pl.BlockSpec(a.shape, lambda b, t: (0,) * a.ndim,
                                  pipeline_mode=pl.Buffered(1))
    tab = pl.BlockSpec((tm, LANES), lambda b, t: (t, 0))
    rows = lambda w: pl.BlockSpec((None, tm, w), lambda b, t: (b, t, 0))
    cols = lambda h: pl.BlockSpec((None, h, tm), lambda b, t: (b, 0, t))
    qk_d_w = DSA_W + HEAD_DIM
    return pl.pallas_call(
        functools.partial(_input_kernel, idx_scale=idx_scale),
        grid=(B, S // tm),
        in_specs=[rows(D), full(gain), full(wf), full(wd), full(wi), full(wm), full(wg),
                  full(bm), full(bg)] + [tab] * 6,
        out_specs=[rows(2 * FOX_W), cols(FOX_W), rows(qk_d_w), cols(HEAD_DIM),
                   rows(wi.shape[1]), rows(wm.shape[1]), rows(wg.shape[1])],
        out_shape=[jax.ShapeDtypeStruct((B, S, 2 * FOX_W), MM_DTYPE),
                   jax.ShapeDtypeStruct((B, FOX_W, S), MM_DTYPE),
                   jax.ShapeDtypeStruct((B, S, qk_d_w), MM_DTYPE),
                   jax.ShapeDtypeStruct((B, HEAD_DIM, S), MM_DTYPE),
                   jax.ShapeDtypeStruct((B, S, wi.shape[1]), MM_DTYPE),
                   jax.ShapeDtypeStruct((B, S, wm.shape[1]), F32),
                   jax.ShapeDtypeStruct((B, S, wg.shape[1]), MM_DTYPE)],
        scratch_shapes=[pltpu.VMEM((SUBLANES, LANES), F32)],
        compiler_params=pltpu.CompilerParams(
            dimension_semantics=("arbitrary", "arbitrary"), vmem_limit_bytes=VMEM_LIMIT),
        name="input_stage",
    )(x, gain, wf, wd, wi, wm, wg, bm, bg, *tabs)


def _flash_attend(i, qk_fn, mask_fn, pv_fn, o_ref, scratch, heads, tq, side_fn=None):
    s_buf, p_buf, m_s, l_s, a_s, acc_s = scratch
    m_s[...] = jnp.full(m_s.shape, NEG_BIG, F32)
    l_s[...] = jnp.zeros(l_s.shape, F32)
    a_s[...] = jnp.ones(a_s.shape, F32)
    acc_s[...] = jnp.zeros(acc_s.shape, F32)
    p_buf[1] = jnp.zeros(p_buf.shape[1:], p_buf.dtype)
    s_buf[0] = qk_fn(0)

    def lagged_values(c, slot):
        acc_s[...] = a_s[0:1, :] * acc_s[...] + pv_fn(c, p_buf[slot])

    def stage(c, cur, last):
        lagged_values(jnp.maximum(c - 1, 0), 1 - cur)
        if not last:
            s_buf[1 - cur] = qk_fn(c + 1)
        if side_fn is not None:
            side_fn(c, last)
        s = mask_fn(c, s_buf[cur], last)
        m_old = m_s[0:1, :]
        m_new = jnp.maximum(m_old, jnp.max(s, axis=0, keepdims=True))
        alpha = jnp.exp2(m_old - m_new)
        p = jnp.exp2(s - m_new)
        l_s[0:1, :] = alpha * l_s[0:1, :] + jnp.sum(p, axis=0, keepdims=True)
        p_buf[cur] = p.astype(p_buf.dtype)
        a_s[0:1, :] = alpha
        m_s[0:1, :] = m_new

    def group(j, _):
        for r in range(STAGES_PER_TRIP):
            stage(STAGES_PER_TRIP * j + r, r % 2, False)
        return 0

    lax.fori_loop(0, i // STAGES_PER_TRIP, group, 0)
    done = (i // STAGES_PER_TRIP) * STAGES_PER_TRIP

    for left in range(STAGES_PER_TRIP):
        @pl.when(i - done == left)
        def _(left=left):
            for r in range(left):
                stage(done + r, r % 2, False)
            stage(i, left % 2, True)
            lagged_values(i, left % 2)

    for h in range(heads):
        cols = slice(h * tq, (h + 1) * tq)
        o_t = acc_s[:, cols] / l_s[0:1, cols]
        o_ref[:, h * HEAD_DIM:(h + 1) * HEAD_DIM] = o_t.T.astype(o_ref.dtype)


STAGES_PER_TRIP = 4


def _flash_scratch(heads, tq):
    w = heads * tq
    return [pltpu.VMEM((2, tq, w), F32),
            pltpu.VMEM((2, tq, w), MM_DTYPE),
            pltpu.VMEM((SUBLANES, w), F32),
            pltpu.VMEM((SUBLANES, w), F32),
            pltpu.VMEM((SUBLANES, w), F32),
            pltpu.VMEM((HEAD_DIM, w), F32)]


def _fox_fns(q_ref, k_ref, vt_ref, m_ref, tq):
    ch = tq
    H = FOX_HEADS
    causal_diag = (lax.broadcasted_iota(jnp.int32, (ch, H * tq), 0)
                   <= lax.broadcasted_iota(jnp.int32, (ch, H * tq), 1) % tq)

    def qk(c):
        start = pl.multiple_of(c * ch, ch)
        parts = []
        for h in range(H):
            hd = slice(h * HEAD_DIM, (h + 1) * HEAD_DIM)
            s = _nt_dot(k_ref[pl.ds(start, ch), hd], q_ref[:, hd])
            parts.append(s - m_ref[pl.ds(start, ch), MISC_F0 + h:MISC_F0 + h + 1])
        return jnp.concatenate(parts, axis=1)

    def mask(c, s, last):
        return jnp.where(causal_diag, s, NEG_BIG) if last else s

    def pv(c, p):
        start = pl.multiple_of(c * ch, ch)
        return jnp.concatenate(
            [_dot(vt_ref[h * HEAD_DIM:(h + 1) * HEAD_DIM, pl.ds(start, ch)],
                  p[:, h * tq:(h + 1) * tq]) for h in range(H)], axis=1)

    return qk, mask, pv


IDX_BITS = 13
SEARCH_MAX_ITERS = 256
SEARCH_FIXED_ITERS = 18
F32_TINY = 1.1754943508222875e-38


def _mixers_kernel(qf_ref, kf_ref, vtf_ref, m_ref, qi_ref, ki_ref, qd_ref, kd_ref, vtd_ref,
                   of_ref, od_ref, sc_ref, ext_ref, *flash_scratch, topk):
    i = pl.program_id(1)
    tq = qi_ref.shape[0]
    ch = tq
    n_ch = i + 1
    key_off = lax.broadcasted_iota(jnp.int32, (ch, tq), 0)
    q_off = lax.broadcasted_iota(jnp.int32, (ch, tq), 1)
    causal_diag = key_off <= q_off

    half_lane = lax.broadcasted_iota(jnp.int32, (tq, LANES), 1) // IDX_DIM
    q_heads = []
    for h in range(IDX_HEADS):
        pair = qi_ref[:, (h // 2) * LANES:(h // 2 + 1) * LANES]
        q_heads.append(jnp.where(half_lane == h % 2, pair, jnp.zeros_like(pair)))
    w_t = m_ref[pl.ds(pl.multiple_of(i * tq, tq), tq), :].T
    ext_ref[0:SUBLANES, :] = jnp.full((SUBLANES, tq), -jnp.inf, F32)
    ext_ref[SUBLANES:2 * SUBLANES, :] = jnp.full((SUBLANES, tq), jnp.inf, F32)

    def score_chunk(c, diag):
        start = pl.multiple_of(c * ch, ch)
        kc = ki_ref[pl.ds(start, ch), :]
        acc = jnp.zeros((ch, tq), F32)
        for h in range(IDX_HEADS):
            acc = acc + jnp.maximum(_nt_dot(kc, q_heads[h]), 0.0) * w_t[h:h + 1, :]
        lo_src = acc
        if diag:
            lo_src = jnp.where(causal_diag, acc, jnp.inf)
            acc = jnp.where(causal_diag, acc, -jnp.inf)
        sc_ref[pl.ds(start, ch), :] = acc
        ext_ref[0:SUBLANES, :] = jnp.maximum(ext_ref[0:SUBLANES, :],
                                             _sublane_fold(acc, jnp.maximum))
        ext_ref[SUBLANES:2 * SUBLANES, :] = jnp.minimum(ext_ref[SUBLANES:2 * SUBLANES, :],
                                                        _sublane_fold(lo_src, jnp.minimum))

    fox_qk, fox_mask, fox_pv = _fox_fns(qf_ref, kf_ref, vtf_ref, m_ref, tq)
    _flash_attend(i, fox_qk, fox_mask, fox_pv, of_ref, flash_scratch, FOX_HEADS, tq,
                  side_fn=score_chunk)

    row_max = jnp.max(ext_ref[0:SUBLANES, :], axis=0, keepdims=True)
    row_min = jnp.min(ext_ref[SUBLANES:2 * SUBLANES, :], axis=0, keepdims=True)

    def count(pred):
        def hits(c):
            start = pl.multiple_of(c * ch, ch)
            hit = pred(sc_ref[pl.ds(start, ch), :], start)
            return _sublane_fold(jnp.where(hit, 1.0, 0.0), jnp.add)
        zero = jnp.zeros((SUBLANES, tq), F32)
        part = lax.fori_loop(0, n_ch // 2, lambda j, cnt: cnt + hits(2 * j) + hits(2 * j + 1), zero)
        part = part + lax.cond(n_ch % 2 == 1, lambda: hits(n_ch - 1), lambda: zero)
        return jnp.sum(part, axis=0, keepdims=True)

    kf = jnp.float32(topk)
    n_causal = (i * tq + 1 + lax.broadcasted_iota(jnp.int32, (1, tq), 1)).astype(F32)
    few = n_causal < kf

    def active(lo, hi, c_lo):
        mid = 0.5 * lo + 0.5 * hi
        return (c_lo > kf) & (mid > lo) & (mid < hi)

    def any_active(state):
        lo, hi, c_lo, it = state[0], state[1], state[2], state[-1]
        busy = jnp.max(jnp.where(active(lo, hi, c_lo), 1.0, 0.0))
        return (busy > 0.5) & (it < SEARCH_MAX_ITERS)

    log_k = jnp.log(kf - 0.5)

    def narrow(state):
        lo, hi, c_lo, c_hi, w_lo, w_hi, last, it = state
        act = active(lo, hi, c_lo)
        mid = 0.5 * lo + 0.5 * hi
        f_lo = (jnp.log(c_lo) - log_k) * w_lo
        f_hi = (log_k - jnp.log(jnp.maximum(c_hi, 0.25))) * w_hi
        guess = lo + (hi - lo) * (f_lo / (f_lo + f_hi))
        cand = jnp.where((guess > lo) & (guess < hi), guess, mid)
        cand = jnp.where((lo < 0.0) & (hi > 0.0), 0.0, cand)
        cand = jnp.where((lo == 0.0) & (hi > F32_TINY), F32_TINY, cand)
        cb = jnp.broadcast_to(cand, (ch, tq))
        cnt = count(lambda s, _: s >= cb)
        up = act & (cnt >= kf)
        dn = act & (cnt < kf)
        w_lo = jnp.where(up, 1.0, jnp.where(dn, jnp.where(last < 0.0, 0.5 * w_lo, 1.0), w_lo))
        w_hi = jnp.where(dn, 1.0, jnp.where(up, jnp.where(last > 0.0, 0.5 * w_hi, 1.0), w_hi))
        last = jnp.where(up, 1.0, jnp.where(dn, -1.0, last))
        return (jnp.where(up, cand, lo), jnp.where(dn, cand, hi),
                jnp.where(up, cnt, c_lo), jnp.where(dn, cnt, c_hi), w_lo, w_hi, last, it + 1)

    above_max = row_max + (row_max - row_min) + 1.0
    ones = jnp.ones((1, tq), F32)
    start_state = (row_min, above_max, jnp.where(few, kf, n_causal), jnp.zeros((1, tq), F32),
                   ones, ones, jnp.zeros((1, tq), F32), jnp.int32(0))
    state = lax.fori_loop(0, SEARCH_FIXED_ITERS, lambda _, st: narrow(st), start_state)
    lo, _, c_lo, c_hi = lax.while_loop(any_active, narrow, state)[:4]

    t = jnp.where(few, -jnp.inf, lo)
    tb = jnp.broadcast_to(t, (ch, tq))

    need = kf - c_hi

    @pl.when(jnp.max(c_lo) > kf)
    def _():
        def idx_step(n, jt):
            cand = jt | (jnp.int32(1) << (IDX_BITS - 1 - n))
            cb = jnp.broadcast_to(cand, (ch, tq))
            below = count(lambda s, start: (s == tb) & (start + key_off < cb))
            return jnp.where(below <= need, cand, jt)
        jt = lax.fori_loop(0, IDX_BITS, idx_step, jnp.zeros((1, tq), jnp.int32))
        jt = jnp.where(c_lo > kf, jt, 2 ** IDX_BITS - 1)
        jb = jnp.broadcast_to(jt, (ch, tq))

        def drop_surplus(c, _):
            start = pl.multiple_of(c * ch, ch)
            s = sc_ref[pl.ds(start, ch), :]
            surplus = (s == tb) & (start + key_off >= jb)
            sc_ref[pl.ds(start, ch), :] = jnp.where(surplus, -jnp.inf, s)
            return 0
        lax.fori_loop(0, n_ch, drop_surplus, 0)

    q_all = jnp.concatenate([qd_ref[:, h * HEAD_DIM:(h + 1) * HEAD_DIM]
                             for h in range(DSA_HEADS)], axis=0)

    def qk(c):
        return _nt_dot(kd_ref[pl.ds(pl.multiple_of(c * ch, ch), ch), :], q_all)

    def mask(c, s, last):
        start = pl.multiple_of(c * ch, ch)
        sel = sc_ref[pl.ds(start, ch), :] >= tb
        if last:
            sel = sel & causal_diag
        return jnp.concatenate([jnp.where(sel, s[:, h * tq:(h + 1) * tq], NEG_BIG)
                                for h in range(DSA_HEADS)], axis=1)

    def pv(c, p):
        return _dot(vtd_ref[:, pl.ds(pl.multiple_of(c * ch, ch), ch)], p)

    _flash_attend(i, qk, mask, pv, od_ref, flash_scratch, DSA_HEADS, tq)


def _token_mixers(qk_f, vt_f, misc, qk_i, qk_d, vt_d, *, tq, topk):
    B, S, _ = qk_d.shape
    assert FOX_HEADS == DSA_HEADS and FOX_W == DSA_W
    blk = lambda w, col: pl.BlockSpec((None, tq, w), lambda b, i: (b, i, col))
    res = lambda w, col: pl.BlockSpec((None, S, w), lambda b, i: (b, 0, col))
    res_t = lambda h: pl.BlockSpec((None, h, S), lambda b, i: (b, 0, 0))
    return pl.pallas_call(
        functools.partial(_mixers_kernel, topk=topk),
        grid=(B, S // tq),
        in_specs=[blk(FOX_W, 0), res(FOX_W, 1), res_t(FOX_W), res(LANES, 0),
                  blk(IDX_W, 0), res(LANES, IDX_W // LANES),
                  blk(DSA_W, 0), res(LANES, DSA_W // LANES), res_t(HEAD_DIM)],
        out_specs=[blk(FOX_W, 0), blk(DSA_W, 0)],
        out_shape=[jax.ShapeDtypeStruct((B, S, FOX_W), MM_DTYPE),
                   jax.ShapeDtypeStruct((B, S, DSA_W), MM_DTYPE)],
        scratch_shapes=[pltpu.VMEM((S, tq), F32),
                        pltpu.VMEM((2 * SUBLANES, tq), F32)]
                       + _flash_scratch(DSA_HEADS, tq),
        compiler_params=pltpu.CompilerParams(
            dimension_semantics=("arbitrary", "arbitrary"), vmem_limit_bytes=VMEM_LIMIT),
        name="token_mixers",
    )(qk_f, qk_f, vt_f, misc, qk_i, qk_i, qk_d, qk_d, vt_d)


def _merge_kernel(of_ref, od_ref, gate_ref, x_ref, wbf_ref, wbd_ref, wo_ref, g_ref, h_ref):
    a = _dot(of_ref[...], wbf_ref[...])
    b = _dot(od_ref[...], wbd_ref[...])
    mixed = (gate_ref[:, 0:D_MODEL].astype(F32) * a
             + gate_ref[:, D_MODEL:2 * D_MODEL].astype(F32) * b)
    y = _dot(mixed.astype(MM_DTYPE), wo_ref[...])
    h_ref[...] = x_ref[...] + _rms(y, g_ref[...])


def _merge_stage(o_f, o_d, gates, x, wbf, wbd, wo, g_post, *, tm):
    N, D = x.shape
    row = lambda w: pl.BlockSpec((tm, w), lambda t: (t, 0))
    full = lambda a: pl.BlockSpec(a.shape, lambda t: (0,) * a.ndim)
    return pl.pallas_call(
        _merge_kernel,
        grid=(N // tm,),
        in_specs=[row(FOX_W), row(DSA_W), row(2 * D), row(D),
                  full(wbf), full(wbd), full(wo), full(g_post)],
        out_specs=row(D),
        out_shape=jax.ShapeDtypeStruct((N, D), F32),
        compiler_params=pltpu.CompilerParams(
            dimension_semantics=("arbitrary",), vmem_limit_bytes=VMEM_LIMIT),
        name="merge_stage",
    )(o_f, o_d, gates, x, wbf, wbd, wo, g_post)


def _ffn_kernel(h_ref, gpre_ref, wg_ref, wu_ref, wd_ref, gpost_ref, o_ref):
    h = h_ref[...]
    v = _rms(h, gpre_ref[...]).astype(MM_DTYPE)
    g = _dot(v, wg_ref[...])
    a = (g * _sigmoid(g) * _dot(v, wu_ref[...])).astype(MM_DTYPE)
    o_ref[...] = h + _rms(_dot(a, wd_ref[...]), gpost_ref[...])


def _ffn_stage(h, g_pre, wg, wu, wd, g_post, *, tm):
    N, D = h.shape
    full = lambda a: pl.BlockSpec(a.shape, lambda t: (0,) * a.ndim,
                                  pipeline_mode=pl.Buffered(1))
    row = pl.BlockSpec((tm, D), lambda t: (t, 0))
    return pl.pallas_call(
        _ffn_kernel,
        grid=(N // tm,),
        in_specs=[row, full(g_pre), full(wg), full(wu), full(wd), full(g_post)],
        out_specs=row,
        out_shape=jax.ShapeDtypeStruct((N, D), F32),
        compiler_params=pltpu.CompilerParams(
            dimension_semantics=("arbitrary",), vmem_limit_bytes=VMEM_LIMIT),
        name="ffn_stage",
    )(h, g_pre, wg, wu, wd, g_post)


def _rope_tables(S, rot, period):
    half = rot // 2
    inv_freq = jnp.float32(ROPE_THETA) ** (-jnp.arange(half, dtype=F32) * 2.0 / rot)
    ang = jnp.arange(S).astype(F32)[:, None] * inv_freq[None, :]
    cos, sin = jnp.cos(ang), jnp.sin(ang)
    one = jnp.ones((S, period - rot), F32)
    zero = jnp.zeros((S, period - rot), F32)
    zh = jnp.zeros((S, half), F32)
    reps = LANES // period
    cos_t = jnp.tile(jnp.concatenate([cos, cos, one], axis=1), (1, reps))
    sin_a = jnp.tile(jnp.concatenate([-sin, zh, zero], axis=1), (1, reps))
    sin_b = jnp.tile(jnp.concatenate([zh, sin, zero], axis=1), (1, reps))
    return cos_t, sin_a, sin_b


def _split_w_in(w):
    parts, off = [], 0
    for width in IN_WIDTHS:
        parts.append(w[:, off:off + width])
        off += width
    return parts


def _layer(h, p, tabs, *, tm_in, tq_mix, tm_merge, tm_ffn):
    B, S, D = h.shape
    (w_qf, w_kf, w_vf, w_fl, w_qd, w_kd, w_vd, w_qi, w_ki, w_wi, w_gf, w_gd) = _split_w_in(p["w_in"])
    cast = lambda a: a.astype(MM_DTYPE)
    pad_m = jnp.zeros((D, LANES - IDX_HEADS - FOX_HEADS), F32)
    wf = cast(jnp.concatenate([w_qf, w_kf, w_vf], axis=1))
    wd = cast(jnp.concatenate([w_qd, w_kd, w_vd], axis=1))
    wi = cast(jnp.concatenate([w_qi, w_ki, w_ki], axis=1))
    wm = cast(jnp.concatenate([w_wi, w_fl, pad_m], axis=1))
    wg = cast(jnp.concatenate([w_gf, w_gd], axis=1))
    bm = jnp.zeros((1, LANES), F32).at[0, MISC_F0:MISC_F0 + FOX_HEADS].set(p["b_forget"].astype(F32))
    bg = p["b_gate"].astype(F32).reshape(1, 2 * D)
    row = lambda a: a.astype(F32).reshape(1, D)

    qk_f, vt_f, qk_d, vt_d, qk_i, misc, gates = _input_stage(
        h, row(p["norm_mix_pre"]), wf, wd, wi, wm, wg, bm, bg, tabs, tm=tm_in)

    o_f, o_d = _token_mixers(qk_f, vt_f, misc, qk_i, qk_d, vt_d, tq=tq_mix,
                             topk=min(TOPK_MAX, S // 4))

    N = B * S
    h1 = _merge_stage(o_f.reshape(N, FOX_W), o_d.reshape(N, DSA_W), gates.reshape(N, 2 * D),
                      h.reshape(N, D), cast(p["w_branch_fox"]), cast(p["w_branch_dsa"]),
                      cast(p["w_out"]), row(p["norm_mix_post"]), tm=tm_merge)
    h2 = _ffn_stage(h1, row(p["norm_ffn_pre"]), cast(p["w_ffn_gate"]), cast(p["w_ffn_up"]),
                    cast(p["w_ffn_down"]), row(p["norm_ffn_post"]), tm=tm_ffn)
    return h2.reshape(B, S, D)


def kernel(x, norm_mix_pre, w_in, b_forget, b_gate, w_branch_fox, w_branch_dsa, w_out,
           norm_mix_post, norm_ffn_pre, w_ffn_gate, w_ffn_up, w_ffn_down, norm_ffn_post):
    B, S, D = x.shape
    params = dict(norm_mix_pre=norm_mix_pre, w_in=w_in, b_forget=b_forget, b_gate=b_gate,
                  w_branch_fox=w_branch_fox, w_branch_dsa=w_branch_dsa, w_out=w_out,
                  norm_mix_post=norm_mix_post, norm_ffn_pre=norm_ffn_pre, w_ffn_gate=w_ffn_gate,
                  w_ffn_up=w_ffn_up, w_ffn_down=w_ffn_down, norm_ffn_post=norm_ffn_post)
    tabs = (_rope_tables(S, HEAD_DIM // ROT_FRAC_DEN, HEAD_DIM)
            + _rope_tables(S, IDX_DIM // ROT_FRAC_DEN, IDX_DIM))
    tiles = dict(tm_in=min(1024, S), tq_mix=256,
                 tm_merge=min(1024, B * S), tm_ffn=min(1024, B * S))
    h = x
    for l in range(w_in.shape[0]):
        h = _layer(h, {k: v[l] for k, v in params.items()}, tabs, **tiles)
    return h
```

```python
import functools

import jax
import jax.numpy as jnp
from jax import lax
from jax.experimental import pallas as pl
from jax.experimental.pallas import tpu as pltpu

D_MODEL = 1024
HEAD_DIM = 128
FOX_HEADS = 4
DSA_HEADS = 4
IDX_HEADS = 8
IDX_DIM = 64
ROT_FRAC_DEN = 4
ROPE_THETA = 500000.0
TOPK_MAX = 256
D_FF = 2816
RMS_EPS = 1e-6
FOX_W = FOX_HEADS * HEAD_DIM
DSA_W = DSA_HEADS * HEAD_DIM
IDX_W = IDX_HEADS * IDX_DIM
IN_WIDTHS = (FOX_W, FOX_W, FOX_W, FOX_HEADS, DSA_W, HEAD_DIM, HEAD_DIM,
             IDX_W, IDX_DIM, IDX_HEADS, D_MODEL, D_MODEL)

LANES = 128
SUBLANES = 8
VMEM_LIMIT = 56 * 1024 * 1024
MM_DTYPE = jnp.bfloat16
NEG_BIG = -1e30
LOG2E = 1.4426950408889634
Q_SCALE = HEAD_DIM ** -0.5 * LOG2E

F32 = jnp.float32


def _nt_dot(a, b):
    return lax.dot_general(a, b, (((1,), (1,)), ((), ())), preferred_element_type=F32)


def _dot(a, b):
    return jnp.dot(a, b, preferred_element_type=F32)


def _rms(x, g):
    return x * lax.rsqrt(jnp.mean(x * x, axis=-1, keepdims=True) + RMS_EPS) * g


def _sigmoid(z):
    return 1.0 / (1.0 + jnp.exp(-z))


def _sublane_fold(x, op):
    acc = x[0:SUBLANES, :]
    for j in range(1, x.shape[0] // SUBLANES):
        acc = op(acc, x[j * SUBLANES:(j + 1) * SUBLANES, :])
    return acc


MISC_F0 = IDX_HEADS


def _rope(seg, cos, sin_a, sin_b, half):
    return (seg * cos
            + pltpu.roll(seg, LANES - half, axis=1) * sin_a
            + pltpu.roll(seg, half, axis=1) * sin_b)


def _input_kernel(x_ref, g_ref, wf_ref, wd_ref, wi_ref, wm_ref, wg_ref, bm_ref, bg_ref,
                  cd_ref, sad_ref, sbd_ref, ci_ref, sai_ref, sbi_ref,
                  f_ref, vtf_ref, d_ref, vtd_ref, i_ref, m_ref, gate_ref, carry_ref, *, idx_scale):
    tm = x_ref.shape[0]
    u = _rms(x_ref[...], g_ref[...]).astype(MM_DTYPE)

    yf = _dot(u, wf_ref[...])
    f_ref[:, 0:FOX_W] = (yf[:, 0:FOX_W] * Q_SCALE).astype(f_ref.dtype)
    f_ref[:, FOX_W:2 * FOX_W] = yf[:, FOX_W:2 * FOX_W].astype(f_ref.dtype)
    for h in range(FOX_HEADS):
        v = yf[:, 2 * FOX_W + h * HEAD_DIM:2 * FOX_W + (h + 1) * HEAD_DIM]
        vtf_ref[h * HEAD_DIM:(h + 1) * HEAD_DIM, :] = v.T.astype(vtf_ref.dtype)

    yd = _dot(u, wd_ref[...])
    cd, sad, sbd = cd_ref[...], sad_ref[...], sbd_ref[...]
    for j in range(DSA_HEADS + 1):
        seg = _rope(yd[:, j * LANES:(j + 1) * LANES], cd, sad, sbd, HEAD_DIM // ROT_FRAC_DEN // 2)
        if j < DSA_HEADS:
            seg = seg * Q_SCALE
        d_ref[:, j * LANES:(j + 1) * LANES] = seg.astype(d_ref.dtype)
    vtd_ref[...] = yd[:, (DSA_HEADS + 1) * LANES:].T.astype(vtd_ref.dtype)

    yi = _dot(u, wi_ref[...])
    ci, sai, sbi = ci_ref[...], sai_ref[...], sbi_ref[...]
    for j in range(yi.shape[1] // LANES):
        seg = _rope(yi[:, j * LANES:(j + 1) * LANES], ci, sai, sbi, IDX_DIM // ROT_FRAC_DEN // 2)
        i_ref[:, j * LANES:(j + 1) * LANES] = seg.astype(i_ref.dtype)

    gate_ref[...] = _sigmoid(_dot(u, wg_ref[...]) + bg_ref[...]).astype(gate_ref.dtype)

    ym = _dot(u, wm_ref[...])
    col = lax.broadcasted_iota(jnp.int32, ym.shape, 1)
    row = lax.broadcasted_iota(jnp.int32, ym.shape, 0)
    is_f = (col >= MISC_F0) & (col < MISC_F0 + FOX_HEADS)
    z = ym + bm_ref[...]
    log_f = jnp.where(is_f, jnp.minimum(z, 0.0) - jnp.log1p(jnp.exp(-jnp.abs(z))), 0.0)

    @pl.when(pl.program_id(1) == 0)
    def _():
        carry_ref[...] = jnp.zeros_like(carry_ref)

    c = log_f
    shift = 1
    while shift < tm:
        c = c + jnp.where(row >= shift, pltpu.roll(c, shift, axis=0), 0.0)
        shift *= 2
    c = c + carry_ref[0:1, :]
    carry_ref[0:1, :] = c[tm - 1:tm, :]
    m_ref[...] = jnp.where(is_f, c * LOG2E, ym * idx_scale)


def _input_stage(x, gain, wf, wd, wi, wm, wg, bm, bg, tabs, *, tm):
    B, S, D = x.shape
    idx_scale = (IDX_DIM ** -0.5) * (IDX_HEADS ** -0.5)
    full = lambda a: pl.BlockSpec(a.shape, lambda b, t: (0,) * a.ndim,
                                  pipeline_mode=pl.Buffered(1))
    tab = pl.BlockSpec((tm, LANES), lambda b, t: (t, 0))
    rows = lambda w: pl.BlockSpec((None, tm, w), lambda b, t: (b, t, 0))
    cols = lambda h: pl.BlockSpec((None, h, tm), lambda b, t: (b, 0, t))
    qk_d_w = DSA_W + HEAD_DIM
    return pl.pallas_call(
        functools.partial(_input_kernel, idx_scale=idx_scale),
        grid=(B, S // tm),
        in_specs=[rows(D), full(gain), full(wf), full(wd), full(wi), full(wm), full(wg),
                  full(bm), full(bg)] + [tab] * 6,
        out_specs=[rows(2 * FOX_W), cols(FOX_W), rows(qk_d_w), cols(HEAD_DIM),
                   rows(wi.shape[1]), rows(wm.shape[1]), rows(wg.shape[1])],
        out_shape=[jax.ShapeDtypeStruct((B, S, 2 * FOX_W), MM_DTYPE),
                   jax.ShapeDtypeStruct((B, FOX_W, S), MM_DTYPE),
                   jax.ShapeDtypeStruct((B, S, qk_d_w), MM_DTYPE),
                   jax.ShapeDtypeStruct((B, HEAD_DIM, S), MM_DTYPE),
                   jax.ShapeDtypeStruct((B, S, wi.shape[1]), MM_DTYPE),
                   jax.ShapeDtypeStruct((B, S, wm.shape[1]), F32),
                   jax.ShapeDtypeStruct((B, S, wg.shape[1]), MM_DTYPE)],
        scratch_shapes=[pltpu.VMEM((SUBLANES, LANES), F32)],
        compiler_params=pltpu.CompilerParams(
            dimension_semantics=("arbitrary", "arbitrary"), vmem_limit_bytes=VMEM_LIMIT),
        name="input_stage",
    )(x, gain, wf, wd, wi, wm, wg, bm, bg, *tabs)


def _flash_attend(i, qk_fn, mask_fn, pv_fn, o_ref, scratch, heads, tq, side_fn=None):
    s_buf, p_buf, m_s, l_s, a_s, acc_s = scratch
    m_s[...] = jnp.full(m_s.shape, NEG_BIG, F32)
    l_s[...] = jnp.zeros(l_s.shape, F32)
    a_s[...] = jnp.ones(a_s.shape, F32)
    acc_s[...] = jnp.zeros(acc_s.shape, F32)
    p_buf[1] = jnp.zeros(p_buf.shape[1:], p_buf.dtype)
    s_buf[0] = qk_fn(0)

    def lagged_values(c, slot):
        acc_s[...] = a_s[0:1, :] * acc_s[...] + pv_fn(c, p_buf[slot])

    def stage(c, cur, last):
        lagged_values(jnp.maximum(c - 1, 0), 1 - cur)
        if not last:
            s_buf[1 - cur] = qk_fn(c + 1)
        if side_fn is not None:
            side_fn(c, last)
        s = mask_fn(c, s_buf[cur], last)
        m_old = m_s[0:1, :]
        m_new = jnp.maximum(m_old, jnp.max(s, axis=0, keepdims=True))
        alpha = jnp.exp2(m_old - m_new)
        p = jnp.exp2(s - m_new)
        l_s[0:1, :] = alpha * l_s[0:1, :] + jnp.sum(p, axis=0, keepdims=True)
        p_buf[cur] = p.astype(p_buf.dtype)
        a_s[0:1, :] = alpha
        m_s[0:1, :] = m_new

    def group(j, _):
        for r in range(STAGES_PER_TRIP):
            stage(STAGES_PER_TRIP * j + r, r % 2, False)
        return 0

    lax.fori_loop(0, i // STAGES_PER_TRIP, group, 0)
    done = (i // STAGES_PER_TRIP) * STAGES_PER_TRIP

    for left in range(STAGES_PER_TRIP):
        @pl.when(i - done == left)
        def _(left=left):
            for r in range(left):
                stage(done + r, r % 2, False)
            stage(i, left % 2, True)
            lagged_values(i, left % 2)

    for h in range(heads):
        cols = slice(h * tq, (h + 1) * tq)
        o_t = acc_s[:, cols] / l_s[0:1, cols]
        o_ref[:, h * HEAD_DIM:(h + 1) * HEAD_DIM] = o_t.T.astype(o_ref.dtype)


STAGES_PER_TRIP = 4


def _flash_scratch(heads, tq):
    w = heads * tq
    return [pltpu.VMEM((2, tq, w), F32),
            pltpu.VMEM((2, tq, w), MM_DTYPE),
            pltpu.VMEM((SUBLANES, w), F32),
            pltpu.VMEM((SUBLANES, w), F32),
            pltpu.VMEM((SUBLANES, w), F32),
            pltpu.VMEM((HEAD_DIM, w), F32)]


def _fox_fns(q_ref, k_ref, vt_ref, m_ref, tq):
    ch = tq
    H = FOX_HEADS
    causal_diag = (lax.broadcasted_iota(jnp.int32, (ch, H * tq), 0)
                   <= lax.broadcasted_iota(jnp.int32, (ch, H * tq), 1) % tq)

    def qk(c):
        start = pl.multiple_of(c * ch, ch)
        parts = []
        for h in range(H):
            hd = slice(h * HEAD_DIM, (h + 1) * HEAD_DIM)
            s = _nt_dot(k_ref[pl.ds(start, ch), hd], q_ref[:, hd])
            parts.append(s - m_ref[pl.ds(start, ch), MISC_F0 + h:MISC_F0 + h + 1])
        return jnp.concatenate(parts, axis=1)

    def mask(c, s, last):
        return jnp.where(causal_diag, s, NEG_BIG) if last else s

    def pv(c, p):
        start = pl.multiple_of(c * ch, ch)
        return jnp.concatenate(
            [_dot(vt_ref[h * HEAD_DIM:(h + 1) * HEAD_DIM, pl.ds(start, ch)],
                  p[:, h * tq:(h + 1) * tq]) for h in range(H)], axis=1)

    return qk, mask, pv


IDX_BITS = 13
SEARCH_MAX_ITERS = 256
SEARCH_FIXED_ITERS = 18
F32_TINY = 1.1754943508222875e-38


def _mixers_kernel(qf_ref, kf_ref, vtf_ref, m_ref, qi_ref, ki_ref, qd_ref, kd_ref, vtd_ref,
                   of_ref, od_ref, sc_ref, ext_ref, *flash_scratch, topk):
    i = pl.program_id(1)
    tq = qi_ref.shape[0]
    ch = tq
    n_ch = i + 1
    key_off = lax.broadcasted_iota(jnp.int32, (ch, tq), 0)
    q_off = lax.broadcasted_iota(jnp.int32, (ch, tq), 1)
    causal_diag = key_off <= q_off

    half_lane = lax.broadcasted_iota(jnp.int32, (tq, LANES), 1) // IDX_DIM
    q_heads = []
    for h in range(IDX_HEADS):
        pair = qi_ref[:, (h // 2) * LANES:(h // 2 + 1) * LANES]
        q_heads.append(jnp.where(half_lane == h % 2, pair, jnp.zeros_like(pair)))
    w_t = m_ref[pl.ds(pl.multiple_of(i * tq, tq), tq), :].T
    ext_ref[0:SUBLANES, :] = jnp.full((SUBLANES, tq), -jnp.inf, F32)
    ext_ref[SUBLANES:2 * SUBLANES, :] = jnp.full((SUBLANES, tq), jnp.inf, F32)

    def score_chunk(c, diag):
        start = pl.multiple_of(c * ch, ch)
        kc = ki_ref[pl.ds(start, ch), :]
        acc = jnp.zeros((ch, tq), F32)
        for h in range(IDX_HEADS):
            acc = acc + jnp.maximum(_nt_dot(kc, q_heads[h]), 0.0) * w_t[h:h + 1, :]
        lo_src = acc
        if diag:
            lo_src = jnp.where(causal_diag, acc, jnp.inf)
            acc = jnp.where(causal_diag, acc, -jnp.inf)
        sc_ref[pl.ds(start, ch), :] = acc
        ext_ref[0:SUBLANES, :] = jnp.maximum(ext_ref[0:SUBLANES, :],
                                             _sublane_fold(acc, jnp.maximum))
        ext_ref[SUBLANES:2 * SUBLANES, :] = jnp.minimum(ext_ref[SUBLANES:2 * SUBLANES, :],
                                                        _sublane_fold(lo_src, jnp.minimum))

    fox_qk, fox_mask, fox_pv = _fox_fns(qf_ref, kf_ref, vtf_ref, m_ref, tq)
    _flash_attend(i, fox_qk, fox_mask, fox_pv, of_ref, flash_scratch, FOX_HEADS, tq,
                  side_fn=score_chunk)

    row_max = jnp.max(ext_ref[0:SUBLANES, :], axis=0, keepdims=True)
    row_min = jnp.min(ext_ref[SUBLANES:2 * SUBLANES, :], axis=0, keepdims=True)

    def count(pred):
        def hits(c):
            start = pl.multiple_of(c * ch, ch)
            hit = pred(sc_ref[pl.ds(start, ch), :], start)
            return _sublane_fold(jnp.where(hit, 1.0, 0.0), jnp.add)
        zero = jnp.zeros((SUBLANES, tq), F32)
        part = lax.fori_loop(0, n_ch // 2, lambda j, cnt: cnt + hits(2 * j) + hits(2 * j + 1), zero)
        part = part + lax.cond(n_ch % 2 == 1, lambda: hits(n_ch - 1), lambda: zero)
        return jnp.sum(part, axis=0, keepdims=True)

    kf = jnp.float32(topk)
    n_causal = (i * tq + 1 + lax.broadcasted_iota(jnp.int32, (1, tq), 1)).astype(F32)
    few = n_causal < kf

    def active(lo, hi, c_lo):
        mid = 0.5 * lo + 0.5 * hi
        return (c_lo > kf) & (mid > lo) & (mid < hi)

    def any_active(state):
        lo, hi, c_lo, it = state[0], state[1], state[2], state[-1]
        busy = jnp.max(jnp.where(active(lo, hi, c_lo), 1.0, 0.0))
        return (busy > 0.5) & (it < SEARCH_MAX_ITERS)

    log_k = jnp.log(kf - 0.5)

    def narrow(state):
        lo, hi, c_lo, c_hi, w_lo, w_hi, last, it = state
        act = active(lo, hi, c_lo)
        mid = 0.5 * lo + 0.5 * hi
        f_lo = (jnp.log(c_lo) - log_k) * w_lo
        f_hi = (log_k - jnp.log(jnp.maximum(c_hi, 0.25))) * w_hi
        guess = lo + (hi - lo) * (f_lo / (f_lo + f_hi))
        cand = jnp.where((guess > lo) & (guess < hi), guess, mid)
        cand = jnp.where((lo < 0.0) & (hi > 0.0), 0.0, cand)
        cand = jnp.where((lo == 0.0) & (hi > F32_TINY), F32_TINY, cand)
        cb = jnp.broadcast_to(cand, (ch, tq))
        cnt = count(lambda s, _: s >= cb)
        up = act & (cnt >= kf)
        dn = act & (cnt < kf)
        w_lo = jnp.where(up, 1.0, jnp.where(dn, jnp.where(last < 0.0, 0.5 * w_lo, 1.0), w_lo))
        w_hi = jnp.where(dn, 1.0, jnp.where(up, jnp.where(last > 0.0, 0.5 * w_hi, 1.0), w_hi))
        last = jnp.where(up, 1.0, jnp.where(dn, -1.0, last))
        return (jnp.where(up, cand, lo), jnp.where(dn, cand, hi),
                jnp.where(up, cnt, c_lo), jnp.where(dn, cnt, c_hi), w_lo, w_hi, last, it + 1)

    above_max = row_max + (row_max - row_min) + 1.0
    ones = jnp.ones((1, tq), F32)
    start_state = (row_min, above_max, jnp.where(few, kf, n_causal), jnp.zeros((1, tq), F32),
                   ones, ones, jnp.zeros((1, tq), F32), jnp.int32(0))
    state = lax.fori_loop(0, SEARCH_FIXED_ITERS, lambda _, st: narrow(st), start_state)
    lo, _, c_lo, c_hi = lax.while_loop(any_active, narrow, state)[:4]

    t = jnp.where(few, -jnp.inf, lo)
    tb = jnp.broadcast_to(t, (ch, tq))

    need = kf - c_hi

    @pl.when(jnp.max(c_lo) > kf)
    def _():
        def idx_step(n, jt):
            cand = jt | (jnp.int32(1) << (IDX_BITS - 1 - n))
            cb = jnp.broadcast_to(cand, (ch, tq))
            below = count(lambda s, start: (s == tb) & (start + key_off < cb))
            return jnp.where(below <= need, cand, jt)
        jt = lax.fori_loop(0, IDX_BITS, idx_step, jnp.zeros((1, tq), jnp.int32))
        jt = jnp.where(c_lo > kf, jt, 2 ** IDX_BITS - 1)
        jb = jnp.broadcast_to(jt, (ch, tq))

        def drop_surplus(c, _):
            start = pl.multiple_of(c * ch, ch)
            s = sc_ref[pl.ds(start, ch), :]
            surplus = (s == tb) & (start + key_off >= jb)
            sc_ref[pl.ds(start, ch), :] = jnp.where(surplus, -jnp.inf, s)
            return 0
        lax.fori_loop(0, n_ch, drop_surplus, 0)

    q_all = jnp.concatenate([qd_ref[:, h * HEAD_DIM:(h + 1) * HEAD_DIM]
                             for h in range(DSA_HEADS)], axis=0)

    def qk(c):
        return _nt_dot(kd_ref[pl.ds(pl.multiple_of(c * ch, ch), ch), :], q_all)

    def mask(c, s, last):
        start = pl.multiple_of(c * ch, ch)
        sel = sc_ref[pl.ds(start, ch), :] >= tb
        if last:
            sel = sel & causal_diag
        return jnp.concatenate([jnp.where(sel, s[:, h * tq:(h + 1) * tq], NEG_BIG)
                                for h in range(DSA_HEADS)], axis=1)

    def pv(c, p):
        return _dot(vtd_ref[:, pl.ds(pl.multiple_of(c * ch, ch), ch)], p)

    _flash_attend(i, qk, mask, pv, od_ref, flash_scratch, DSA_HEADS, tq)


def _token_mixers(qk_f, vt_f, misc, qk_i, qk_d, vt_d, *, tq, topk):
    B, S, _ = qk_d.shape
    assert FOX_HEADS == DSA_HEADS and FOX_W == DSA_W
    blk = lambda w, col: pl.BlockSpec((None, tq, w), lambda b, i: (b, i, col))
    res = lambda w, col: pl.BlockSpec((None, S, w), lambda b, i: (b, 0, col))
    res_t = lambda h: pl.BlockSpec((None, h, S), lambda b, i: (b, 0, 0))
    return pl.pallas_call(
        functools.partial(_mixers_kernel, topk=topk),
        grid=(B, S // tq),
        in_specs=[blk(FOX_W, 0), res(FOX_W, 1), res_t(FOX_W), res(LANES, 0),
                  blk(IDX_W, 0), res(LANES, IDX_W // LANES),
                  blk(DSA_W, 0), res(LANES, DSA_W // LANES), res_t(HEAD_DIM)],
        out_specs=[blk(FOX_W, 0), blk(DSA_W, 0)],
        out_shape=[jax.ShapeDtypeStruct((B, S, FOX_W), MM_DTYPE),
                   jax.ShapeDtypeStruct((B, S, DSA_W), MM_DTYPE)],
        scratch_shapes=[pltpu.VMEM((S, tq), F32),
                        pltpu.VMEM((2 * SUBLANES, tq), F32)]
                       + _flash_scratch(DSA_HEADS, tq),
        compiler_params=pltpu.CompilerParams(
            dimension_semantics=("arbitrary", "arbitrary"), vmem_limit_bytes=VMEM_LIMIT),
        name="token_mixers",
    )(qk_f, qk_f, vt_f, misc, qk_i, qk_i, qk_d, qk_d, vt_d)


def _merge_kernel(of_ref, od_ref, gate_ref, x_ref, wbf_ref, wbd_ref, wo_ref, g_ref, h_ref):
    a = _dot(of_ref[...], wbf_ref[...])
    b = _dot(od_ref[...], wbd_ref[...])
    mixed = (gate_ref[:, 0:D_MODEL].astype(F32) * a
             + gate_ref[:, D_MODEL:2 * D_MODEL].astype(F32) * b)
    y = _dot(mixed.astype(MM_DTYPE), wo_ref[...])
    h_ref[...] = x_ref[...] + _rms(y, g_ref[...])


def _merge_stage(o_f, o_d, gates, x, wbf, wbd, wo, g_post, *, tm):
    N, D = x.shape
    row = lambda w: pl.BlockSpec((tm, w), lambda t: (t, 0))
    full = lambda a: pl.BlockSpec(a.shape, lambda t: (0,) * a.ndim)
    return pl.pallas_call(
        _merge_kernel,
        grid=(N // tm,),
        in_specs=[row(FOX_W), row(DSA_W), row(2 * D), row(D),
                  full(wbf), full(wbd), full(wo), full(g_post)],
        out_specs=row(D),
        out_shape=jax.ShapeDtypeStruct((N, D), F32),
        compiler_params=pltpu.CompilerParams(
            dimension_semantics=("arbitrary",), vmem_limit_bytes=VMEM_LIMIT),
        name="merge_stage",
    )(o_f, o_d, gates, x, wbf, wbd, wo, g_post)


def _merge_ffn_kernel(of_ref, od_ref, gate_ref, x_ref, wbf_ref, wbd_ref, wo_ref, gmix_ref,
                      gpre_ref, wg_ref, wu_ref, wd_ref, gpost_ref, o_ref):
    a = _dot(of_ref[...], wbf_ref[...])
    b = _dot(od_ref[...], wbd_ref[...])
    mixed = (gate_ref[:, 0:D_MODEL].astype(F32) * a
             + gate_ref[:, D_MODEL:2 * D_MODEL].astype(F32) * b)
    h = x_ref[...] + _rms(_dot(mixed.astype(MM_DTYPE), wo_ref[...]), gmix_ref[...])
    v = _rms(h, gpre_ref[...]).astype(MM_DTYPE)
    g = _dot(v, wg_ref[...])
    act = (g * _sigmoid(g) * _dot(v, wu_ref[...])).astype(MM_DTYPE)
    o_ref[...] = h + _rms(_dot(act, wd_ref[...]), gpost_ref[...])


def _merge_ffn_stage(o_f, o_d, gates, x, wbf, wbd, wo, g_mix, g_pre, wg, wu, wd, g_post, *, tm):
    N, D = x.shape
    row = lambda w: pl.BlockSpec((tm, w), lambda t: (t, 0))
    full = lambda a: pl.BlockSpec(a.shape, lambda t: (0,) * a.ndim,
                                  pipeline_mode=pl.Buffered(1))
    return pl.pallas_call(
        _merge_ffn_kernel,
        grid=(N // tm,),
        in_specs=[row(FOX_W), row(DSA_W), row(2 * D), row(D), full(wbf), full(wbd), full(wo),
                  full(g_mix), full(g_pre), full(wg), full(wu), full(wd), full(g_post)],
        out_specs=row(D),
        out_shape=jax.ShapeDtypeStruct((N, D), F32),
        compiler_params=pltpu.CompilerParams(
            dimension_semantics=("arbitrary",), vmem_limit_bytes=VMEM_LIMIT),
        name="merge_ffn_stage",
    )(o_f, o_d, gates, x, wbf, wbd, wo, g_mix, g_pre, wg, wu, wd, g_post)


def _ffn_kernel(h_ref, gpre_ref, wg_ref, wu_ref, wd_ref, gpost_ref, o_ref):
    h = h_ref[...]
    v = _rms(h, gpre_ref[...]).astype(MM_DTYPE)
    g = _dot(v, wg_ref[...])
    a = (g * _sigmoid(g) * _dot(v, wu_ref[...])).astype(MM_DTYPE)
    o_ref[...] = h + _rms(_dot(a, wd_ref[...]), gpost_ref[...])


def _ffn_stage(h, g_pre, wg, wu, wd, g_post, *, tm):
    N, D = h.shape
    full = lambda a: pl.BlockSpec(a.shape, lambda t: (0,) * a.ndim,
                                  pipeline_mode=pl.Buffered(1))
    row = pl.BlockSpec((tm, D), lambda t: (t, 0))
    return pl.pallas_call(
        _ffn_kernel,
        grid=(N // tm,),
        in_specs=[row, full(g_pre), full(wg), full(wu), full(wd), full(g_post)],
        out_specs=row,
        out_shape=jax.ShapeDtypeStruct((N, D), F32),
        compiler_params=pltpu.CompilerParams(
            dimension_semantics=("arbitrary",), vmem_limit_bytes=VMEM_LIMIT),
        name="ffn_stage",
    )(h, g_pre, wg, wu, wd, g_post)


def _rope_tables(S, rot, period):
    half = rot // 2
    inv_freq = jnp.float32(ROPE_THETA) ** (-jnp.arange(half, dtype=F32) * 2.0 / rot)
    ang = jnp.arange(S).astype(F32)[:, None] * inv_freq[None, :]
    cos, sin = jnp.cos(ang), jnp.sin(ang)
    one = jnp.ones((S, period - rot), F32)
    zero = jnp.zeros((S, period - rot), F32)
    zh = jnp.zeros((S, half), F32)
    reps = LANES // period
    cos_t = jnp.tile(jnp.concatenate([cos, cos, one], axis=1), (1, reps))
    sin_a = jnp.tile(jnp.concatenate([-sin, zh, zero], axis=1), (1, reps))
    sin_b = jnp.tile(jnp.concatenate([zh, sin, zero], axis=1), (1, reps))
    return cos_t, sin_a, sin_b


def _split_w_in(w):
    parts, off = [], 0
    for width in IN_WIDTHS:
        parts.append(w[:, off:off + width])
        off += width
    return parts


def _layer(h, p, tabs, *, tm_in, tq_mix, tm_merge, tm_ffn):
    B, S, D = h.shape
    (w_qf, w_kf, w_vf, w_fl, w_qd, w_kd, w_vd, w_qi, w_ki, w_wi, w_gf, w_gd) = _split_w_in(p["w_in"])
    cast = lambda a: a.astype(MM_DTYPE)
    pad_m = jnp.zeros((D, LANES - IDX_HEADS - FOX_HEADS), F32)
    wf = cast(jnp.concatenate([w_qf, w_kf, w_vf], axis=1))
    wd = cast(jnp.concatenate([w_qd, w_kd, w_vd], axis=1))
    wi = cast(jnp.concatenate([w_qi, w_ki, w_ki], axis=1))
    wm = cast(jnp.concatenate([w_wi, w_fl, pad_m], axis=1))
    wg = cast(jnp.concatenate([w_gf, w_gd], axis=1))
    bm = jnp.zeros((1, LANES), F32).at[0, MISC_F0:MISC_F0 + FOX_HEADS].set(p["b_forget"].astype(F32))
    bg = p["b_gate"].astype(F32).reshape(1, 2 * D)
    row = lambda a: a.astype(F32).reshape(1, D)

    qk_f, vt_f, qk_d, vt_d, qk_i, misc, gates = _input_stage(
        h, row(p["norm_mix_pre"]), wf, wd, wi, wm, wg, bm, bg, tabs, tm=tm_in)

    o_f, o_d = _token_mixers(qk_f, vt_f, misc, qk_i, qk_d, vt_d, tq=tq_mix,
                             topk=min(TOPK_MAX, S // 4))

    N = B * S
    h2 = _merge_ffn_stage(o_f.reshape(N, FOX_W), o_d.reshape(N, DSA_W), gates.reshape(N, 2 * D),
                          h.reshape(N, D), cast(p["w_branch_fox"]), cast(p["w_branch_dsa"]),
                          cast(p["w_out"]), row(p["norm_mix_post"]), row(p["norm_ffn_pre"]),
                          cast(p["w_ffn_gate"]), cast(p["w_ffn_up"]), cast(p["w_ffn_down"]),
                          row(p["norm_ffn_post"]), tm=tm_merge)
    return h2.reshape(B, S, D)


def kernel(x, norm_mix_pre, w_in, b_forget, b_gate, w_branch_fox, w_branch_dsa, w_out,
           norm_mix_post, norm_ffn_pre, w_ffn_gate, w_ffn_up, w_ffn_down, norm_ffn_post):
    B, S, D = x.shape
    params = dict(norm_mix_pre=norm_mix_pre, w_in=w_in, b_forget=b_forget, b_gate=b_gate,
                  w_branch_fox=w_branch_fox, w_branch_dsa=w_branch_dsa, w_out=w_out,
                  norm_mix_post=norm_mix_post, norm_ffn_pre=norm_ffn_pre, w_ffn_gate=w_ffn_gate,
                  w_ffn_up=w_ffn_up, w_ffn_down=w_ffn_down, norm_ffn_post=norm_ffn_post)
    tabs = (_rope_tables(S, HEAD_DIM // ROT_FRAC_DEN, HEAD_DIM)
            + _rope_tables(S, IDX_DIM // ROT_FRAC_DEN, IDX_DIM))
    tiles = dict(tm_in=min(1024, S), tq_mix=256,
                 tm_merge=min(512, B * S), tm_ffn=min(1024, B * S))
    h = x
    for l in range(w_in.shape[0]):
        h = _layer(h, {k: v[l] for k, v in params.items()}, tabs, **tiles)
    return h
```
